```python
import math
import jax, jax.numpy as jnp
from jax import lax
import numpy as np

D_MODEL = 1024
BATCH = 8
SEQ = 4096
DEPTH = 2

HEAD_DIM = 64
ROT_DIM = HEAD_DIM // 4
ROPE_THETA = 500000.0
RMS_EPS = 1e-6
CONV_WIDTH = D_MODEL // 2
CONV_K = 3
DIFF_HEADS = (D_MODEL // 2) // (2 * HEAD_DIM)
DIFF_VDIM = 2 * HEAD_DIM
DIFF_WIDTH = DIFF_HEADS * 2 * HEAD_DIM
DIFF_EPS = 1e-5
QBLK = 128
EVEN_IN = 3 * CONV_WIDTH + 3 * DIFF_WIDTH
SWA_HEADS = D_MODEL // HEAD_DIM
SWA_GROUP = 8
SWA_KV_HEADS = SWA_HEADS // SWA_GROUP
WINDOW = 128
ODD_IN = (SWA_HEADS + 2 * SWA_KV_HEADS) * HEAD_DIM
D_FF = 4 * D_MODEL
N_EVEN = (DEPTH + 1) // 2
N_ODD = DEPTH // 2

kernel_name = "hybrid_conv_diffattn_swa_sink_trunk"


def rmsnorm(x, w, eps=RMS_EPS):
    xf = x.astype(jnp.float32)
    xf = xf * lax.rsqrt(jnp.mean(xf * xf, axis=-1, keepdims=True) + eps)
    return (xf * w.astype(jnp.float32)).astype(x.dtype)


def rope_tables(positions):
    inv_freq = ROPE_THETA ** (-jnp.arange(0, ROT_DIM, 2, dtype=jnp.float32) / ROT_DIM)
    ang = positions.astype(jnp.float32)[..., None] * inv_freq
    return jnp.cos(ang), jnp.sin(ang)


def apply_partial_rope(x, cos, sin):
    half = ROT_DIM // 2
    extra = x.ndim - 3
    c = cos.reshape(cos.shape[:2] + (1,) * extra + (half,))
    s = sin.reshape(sin.shape[:2] + (1,) * extra + (half,))
    xr = x[..., :ROT_DIM].astype(jnp.float32)
    x1, x2 = xr[..., :half], xr[..., half:]
    rot = jnp.concatenate([x1 * c - x2 * s, x2 * c + x1 * s], axis=-1)
    return jnp.concatenate([rot.astype(x.dtype), x[..., ROT_DIM:]], axis=-1)


def causal_short_conv(u, w):
    S = u.shape[1]
    up = jnp.pad(u, ((0, 0), (CONV_K - 1, 0), (0, 0)))
    return sum(w[i] * up[:, i:i + S] for i in range(CONV_K))


def diff_attention(q, k, v, lam, subln_w, lam_init):
    B, S, H, _, d = q.shape
    nb = S // QBLK
    scale = d ** -0.5
    qb = jnp.moveaxis(q.reshape(B, nb, QBLK, H, 2, d), 1, 0)
    kpos = jnp.arange(S)

    def one_block(args):
        blk, qblk = args
        s = jnp.einsum('bqhcd,bkhcd->bhcqk', qblk, k,
                       preferred_element_type=jnp.float32) * scale
        qpos = blk * QBLK + jnp.arange(QBLK)
        mask = kpos[None, :] <= qpos[:, None]
        p = jax.nn.softmax(jnp.where(mask, s, -jnp.inf), axis=-1)
        a = p[:, :, 0] - lam[None, None, None, None] * p[:, :, 1]
        return jnp.einsum('bhqk,bkhe->bqhe', a.astype(v.dtype), v)

    o = lax.map(one_block, (jnp.arange(nb), qb))
    o = jnp.moveaxis(o, 0, 1).reshape(B, S, H, 2 * d)
    o = rmsnorm(o, subln_w, DIFF_EPS) * (1.0 - lam_init)
    return o.reshape(B, S, H * 2 * d)


def even_mixer(h, cos, sin, w_in, conv_w, lq1, lk1, lq2, lk2, subln_w, w_out, lam_init):
    B, S, _ = h.shape
    proj = h @ w_in
    c0 = CONV_WIDTH
    gb, gc, xc, q, k, v = jnp.split(
        proj, [c0, 2 * c0, 3 * c0, 3 * c0 + DIFF_WIDTH, 3 * c0 + 2 * DIFF_WIDTH], axis=-1)
    conv_out = gb * causal_short_conv(gc * xc, conv_w)
    q = apply_partial_rope(q.reshape(B, S, DIFF_HEADS, 2, HEAD_DIM), cos, sin)
    k = apply_partial_rope(k.reshape(B, S, DIFF_HEADS, 2, HEAD_DIM), cos, sin)
    v = v.reshape(B, S, DIFF_HEADS, DIFF_VDIM)
    f32 = jnp.float32
    lam = (jnp.exp(jnp.sum(lq1.astype(f32) * lk1.astype(f32)))
           - jnp.exp(jnp.sum(lq2.astype(f32) * lk2.astype(f32))) + lam_init)
    diff_out = diff_attention(q, k, v, lam, subln_w, lam_init)
    return jnp.concatenate([conv_out, diff_out], axis=-1) @ w_out


def band(t, nb):
    B, S, KV, d = t.shape
    tp = jnp.pad(t, ((0, 0), (WINDOW, 0), (0, 0), (0, 0)))
    prev = tp[:, :S].reshape(B, nb, WINDOW, KV, d)
    cur = t.reshape(B, nb, WINDOW, KV, d)
    return jnp.concatenate([prev, cur], axis=2)


def sliding_window_attention(q, k, v, sinks):
    B, S, H, d = q.shape
    nb = S // WINDOW
    qb = q.reshape(B, nb, WINDOW, SWA_KV_HEADS, SWA_GROUP, d)
    kb, vb = band(k, nb), band(v, nb)
    s = jnp.einsum('bnqkgd,bnjkd->bnkgqj', qb, kb,
                   preferred_element_type=jnp.float32) * (d ** -0.5)
    i = jnp.arange(WINDOW)[:, None]
    j = jnp.arange(2 * WINDOW)[None, :]
    dist = i + WINDOW - j
    in_band = (dist >= 0) & (dist < WINDOW)
    key_valid = (jnp.arange(nb)[:, None, None] * WINDOW - WINDOW + j[None]) >= 0
    mask = in_band[None] & key_valid
    s = jnp.where(mask[None, :, None, None], s, -jnp.inf)
    sink = sinks.astype(jnp.float32).reshape(SWA_KV_HEADS, SWA_GROUP)[None, None, :, :, None, None]
    sink = jnp.broadcast_to(sink, s.shape[:-1] + (1,))
    p = jax.nn.softmax(jnp.concatenate([s, sink], axis=-1), axis=-1)[..., :-1]
    o = jnp.einsum('bnkgqj,bnjkd->bnqkgd', p.astype(v.dtype), vb)
    return o.reshape(B, S, H * d)


def odd_mixer(h, cos, sin, w_qkv, b_qkv, sinks, w_o, b_o):
    B, S, _ = h.shape
    proj = h @ w_qkv + b_qkv
    nq = SWA_HEADS * HEAD_DIM
    nk = SWA_KV_HEADS * HEAD_DIM
    q, k, v = jnp.split(proj, [nq, nq + nk], axis=-1)
    q = apply_partial_rope(q.reshape(B, S, SWA_HEADS, HEAD_DIM), cos, sin)
    k = apply_partial_rope(k.reshape(B, S, SWA_KV_HEADS, HEAD_DIM), cos, sin)
    v = v.reshape(B, S, SWA_KV_HEADS, HEAD_DIM)
    return sliding_window_attention(q, k, v, sinks) @ w_o + b_o


def squared_relu_mlp(h, w1, w2):
    return jnp.square(jax.nn.relu(h @ w1)) @ w2


def setup_inputs(seed: int = 0) -> dict:
    key = jax.random.key(seed)
    ks = jax.random.split(key, 24)
    f32 = jnp.float32

    def nrm(k, shape, scale):
        return jax.random.normal(k, shape, f32) * scale

    def gain(k, shape):
        return 1.0 + 0.05 * jax.random.normal(k, shape, f32)

    offset = jax.random.randint(ks[1], (BATCH, 1), 0, SEQ, dtype=jnp.int32)
    positions = offset + jnp.arange(SEQ, dtype=jnp.int32)[None, :]
    return {
        "x": nrm(ks[0], (BATCH, SEQ, D_MODEL), 1.0),
        "positions": positions,
        "norm_pre_mix": gain(ks[2], (DEPTH, D_MODEL)),
        "norm_post_mix": gain(ks[3], (DEPTH, D_MODEL)),
        "norm_pre_mlp": gain(ks[4], (DEPTH, D_MODEL)),
        "norm_post_mlp": gain(ks[5], (DEPTH, D_MODEL)),
        "even_w_in": nrm(ks[6], (N_EVEN, D_MODEL, EVEN_IN), D_MODEL ** -0.5),
        "even_conv_w": nrm(ks[7], (N_EVEN, CONV_K, CONV_WIDTH), CONV_K ** -0.5),
        "even_lambda_q1": nrm(ks[8], (N_EVEN, HEAD_DIM), 0.1),
        "even_lambda_k1": nrm(ks[9], (N_EVEN, HEAD_DIM), 0.1),
        "even_lambda_q2": nrm(ks[10], (N_EVEN, HEAD_DIM), 0.1),
        "even_lambda_k2": nrm(ks[11], (N_EVEN, HEAD_DIM), 0.1),
        "even_subln_w": gain(ks[12], (N_EVEN, DIFF_VDIM)),
        "even_w_out": nrm(ks[13], (N_EVEN, D_MODEL, D_MODEL), D_MODEL ** -0.5),
        "odd_w_qkv": nrm(ks[14], (N_ODD, D_MODEL, ODD_IN), D_MODEL ** -0.5),
        "odd_b_qkv": nrm(ks[15], (N_ODD, ODD_IN), 0.02),
        "odd_sinks": nrm(ks[16], (N_ODD, SWA_HEADS), 0.5),
        "odd_w_o": nrm(ks[17], (N_ODD, D_MODEL, D_MODEL), D_MODEL ** -0.5),
        "odd_b_o": nrm(ks[18], (N_ODD, D_MODEL), 0.02),
        "mlp_w1": nrm(ks[19], (DEPTH, D_MODEL, D_FF), D_MODEL ** -0.5),
        "mlp_w2": nrm(ks[20], (DEPTH, D_FF, D_MODEL), D_FF ** -0.5),
    }


def reference(x, positions, norm_pre_mix, norm_post_mix, norm_pre_mlp, norm_post_mlp,
              even_w_in, even_conv_w, even_lambda_q1, even_lambda_k1, even_lambda_q2,
              even_lambda_k2, even_subln_w, even_w_out, odd_w_qkv, odd_b_qkv, odd_sinks,
              odd_w_o, odd_b_o, mlp_w1, mlp_w2):
    cos, sin = rope_tables(positions)
    for l in range(DEPTH):
        h = rmsnorm(x, norm_pre_mix[l])
        if l % 2 == 0:
            e = l // 2
            lam_init = 0.8 - 0.6 * math.exp(-0.3 * l)
            h = even_mixer(h, cos, sin, even_w_in[e], even_conv_w[e], even_lambda_q1[e],
                           even_lambda_k1[e], even_lambda_q2[e], even_lambda_k2[e],
                           even_subln_w[e], even_w_out[e], lam_init)
        else:
            o = l // 2
            h = odd_mixer(h, cos, sin, odd_w_qkv[o], odd_b_qkv[o], odd_sinks[o],
                          odd_w_o[o], odd_b_o[o])
        x = x + rmsnorm(h, norm_post_mix[l])
        h = squared_relu_mlp(rmsnorm(x, norm_pre_mlp[l]), mlp_w1[l], mlp_w2[l])
        x = x + rmsnorm(h, norm_post_mlp[l])
    return x
```

```python
import functools
import math

import jax
import jax.numpy as jnp
from jax import lax
from jax.experimental import pallas as pl
from jax.experimental.pallas import tpu as pltpu

D_MODEL = 1024
HEAD_DIM = 64
ROT_HALF = 8
ROPE_THETA = 500000.0
RMS_EPS = 1e-6
CONV_WIDTH = 512
CONV_K = 3
DIFF_HEADS = 4
DIFF_WIDTH = 512
DIFF_EPS = 1e-5
SWA_HEADS = 16
SWA_GROUP = 8
SWA_KV_HEADS = 2
WINDOW = 128
D_FF = 4096
LANES = 128
SUBLANES = 8

VMEM_LIMIT = 56 * 1024 * 1024

BF16 = jnp.bfloat16
F32 = jnp.float32


def _params(*sem):
    return pltpu.CompilerParams(dimension_semantics=sem, vmem_limit_bytes=VMEM_LIMIT)


def _resident(shape):
    return pl.BlockSpec(shape, lambda *_: (0,) * len(shape), pipeline_mode=pl.Buffered(1))


def _rmsnorm(x, w, eps):
    return x * lax.rsqrt(jnp.mean(x * x, axis=-1, keepdims=True) + eps) * w


def _rope(x, cos_t, sin_t):
    lane = lax.broadcasted_iota(jnp.int32, x.shape, 1) % HEAD_DIM
    first = lane < ROT_HALF
    partner = jnp.where(first, pltpu.roll(x, LANES - ROT_HALF, 1), pltpu.roll(x, ROT_HALF, 1))
    return x * cos_t + partner * jnp.where(first, -sin_t, sin_t)


def _rope_table_kernel(invf_ref, pos_ref, cos_ref, sin_ref):
    pos = pos_ref[...].astype(F32)
    for f in range(ROT_HALF):
        ang = pos * invf_ref[f]
        cos_ref[f] = jnp.cos(ang)
        sin_ref[f] = jnp.sin(ang)


def _rope_tables(positions):
    n = positions.size
    inv_freq = ROPE_THETA ** (-jnp.arange(0, 2 * ROT_HALF, 2, dtype=F32) / (2 * ROT_HALF))
    pos2d = positions.reshape(n // LANES, LANES)
    cos_c, sin_c = pl.pallas_call(
        _rope_table_kernel,
        out_shape=[jax.ShapeDtypeStruct((ROT_HALF, n // LANES, LANES), F32)] * 2,
        in_specs=[pl.BlockSpec(memory_space=pltpu.SMEM),
                  pl.BlockSpec(memory_space=pltpu.VMEM)],
        out_specs=[pl.BlockSpec(memory_space=pltpu.VMEM)] * 2,
        name="rope_tables",
    )(inv_freq, pos2d)

    def expand(t, fill):
        t = t.reshape(ROT_HALF, n).T
        pad = jnp.full((n, HEAD_DIM - 2 * ROT_HALF), fill, F32)
        head = jnp.concatenate([t, t, pad], axis=1)
        return jnp.concatenate([head, head], axis=1)

    return expand(cos_c, 1.0), expand(sin_c, 0.0)


def _proj0_kernel(x_ref, nw_ref, w_ref, cos_ref, sin_ref, cw_ref,
                  conv_ref, q_ref, k_ref, v_ref, carry_ref, *, tiles_per_seq):
    i = pl.program_id(0)
    tm = x_ref.shape[0]
    hb = _rmsnorm(x_ref[...], nw_ref[...], RMS_EPS).astype(BF16)

    def proj(c0, width):
        return jnp.dot(hb, w_ref[:, c0:c0 + width], preferred_element_type=F32)

    c = CONV_WIDTH
    gb = proj(0, c)
    u = proj(c, c) * proj(2 * c, c)

    @pl.when(i % tiles_per_seq == 0)
    def _():
        carry_ref[...] = jnp.zeros_like(carry_ref)

    prev = carry_ref[...]
    row = lax.broadcasted_iota(jnp.int32, u.shape, 0)
    u1 = jnp.where(row == 0, prev[SUBLANES - 1:SUBLANES], pltpu.roll(u, 1, 0))
    u2 = jnp.where(row == 0, prev[SUBLANES - 2:SUBLANES - 1],
                   jnp.where(row == 1, prev[SUBLANES - 1:SUBLANES], pltpu.roll(u, 2, 0)))
    cw = cw_ref[...]
    conv = cw[0:1] * u2 + cw[1:2] * u1 + cw[2:3] * u
    conv_ref[...] = (gb * conv).astype(BF16)
    carry_ref[...] = u[tm - SUBLANES:, :]

    cos_t = cos_ref[...]
    sin_t = sin_ref[...]
    scale = HEAD_DIM ** -0.5
    for j in range(DIFF_WIDTH // LANES):
        qj = proj(3 * c + j * LANES, LANES)
        q_ref[:, j * LANES:(j + 1) * LANES] = (_rope(qj, cos_t, sin_t) * scale).astype(BF16)
        kj = proj(3 * c + DIFF_WIDTH + j * LANES, LANES)
        k_ref[:, j * LANES:(j + 1) * LANES] = _rope(kj, cos_t, sin_t).astype(BF16)
    v_ref[...] = proj(3 * c + 2 * DIFF_WIDTH, DIFF_WIDTH).astype(BF16)


def _proj0(x, nw, w, cos_t, sin_t, conv_w, seq, tm=256):
    n = x.shape[0]
    row = lambda width: pl.BlockSpec((tm, width), lambda i: (i, 0))
    out = jax.ShapeDtypeStruct((n, CONV_WIDTH), BF16)
    return pl.pallas_call(
        functools.partial(_proj0_kernel, tiles_per_seq=seq // tm),
        grid=(n // tm,),
        in_specs=[row(D_MODEL), _resident((1, D_MODEL)), _resident(w.shape),
                  row(LANES), row(LANES), _resident(conv_w.shape)],
        out_specs=[row(CONV_WIDTH)] * 4,
        out_shape=[out] * 4,
        scratch_shapes=[pltpu.VMEM((SUBLANES, CONV_WIDTH), F32)],
        compiler_params=_params("arbitrary"),
        name="proj0",
    )(x, nw, w, cos_t, sin_t, conv_w)


def _diff_attn_kernel(lam_ref, q_ref, k_ref, v_ref, sw_ref, o_ref,
                      qs_ref, m_ref, l_ref, acc_ref, *, lam_init):
    qi = pl.program_id(2)
    tq = q_ref.shape[0]
    tk = tq

    q = q_ref[...].astype(F32)
    lane = lax.broadcasted_iota(jnp.int32, q.shape, 1)
    qs_ref[0:tq, :] = jnp.where(lane < HEAD_DIM, q, 0.0).astype(BF16)
    qs_ref[tq:, :] = jnp.where(lane >= HEAD_DIM, q, 0.0).astype(BF16)
    m_ref[...] = jnp.full_like(m_ref, -jnp.inf)
    l_ref[...] = jnp.zeros_like(l_ref)
    acc_ref[...] = jnp.zeros_like(acc_ref)

    def step(kb, masked):
        start = pl.multiple_of(kb * tk, tk)
        k = k_ref[pl.ds(start, tk), :]
        v = v_ref[pl.ds(start, tk), :]
        s = lax.dot_general(qs_ref[...], k, (((1,), (1,)), ((), ())),
                            preferred_element_type=F32)
        if masked:
            r = lax.broadcasted_iota(jnp.int32, s.shape, 0) % tq
            col = lax.broadcasted_iota(jnp.int32, s.shape, 1)
            s = jnp.where(col <= r, s, -jnp.inf)
        m_prev = m_ref[...]
        m_new = jnp.maximum(m_prev, jnp.max(s, axis=-1, keepdims=True))
        alpha = jnp.exp(m_prev - m_new)
        p = jnp.exp(s - m_new)
        l_ref[...] = alpha * l_ref[...] + jnp.sum(p, axis=-1, keepdims=True)
        acc_ref[...] = alpha * acc_ref[...] + jnp.dot(p.astype(BF16), v,
                                                      preferred_element_type=F32)
        m_ref[...] = m_new

    def body(kb, carry):
        step(kb, False)
        return carry

    lax.fori_loop(0, qi, body, 0)
    step(qi, True)

    lam_v = lam_ref[...]
    lam = (jnp.exp(jnp.sum(lam_v[0:1] * lam_v[1:2], keepdims=True))
           - jnp.exp(jnp.sum(lam_v[2:3] * lam_v[3:4], keepdims=True)) + lam_init)
    o = acc_ref[...] / l_ref[...]
    o = o[0:tq] - lam * o[tq:]
    o = _rmsnorm(o, sw_ref[...], DIFF_EPS) * (1.0 - lam_init)
    o_ref[...] = o.astype(o_ref.dtype)


def _diff_attention(q, k, v, lam_vecs, subln_w, lam_init, batch, seq, tq=256):
    q, k, v = (t.reshape(batch, seq, DIFF_WIDTH) for t in (q, k, v))
    hw = 2 * HEAD_DIM
    out = pl.pallas_call(
        functools.partial(_diff_attn_kernel, lam_init=lam_init),
        grid=(batch, DIFF_HEADS, seq // tq),
        in_specs=[_resident(lam_vecs.shape),
                  pl.BlockSpec((None, tq, hw), lambda b, h, i: (b, i, h)),
                  pl.BlockSpec((None, seq, hw), lambda b, h, i: (b, 0, h)),
                  pl.BlockSpec((None, seq, hw), lambda b, h, i: (b, 0, h)),
                  _resident(subln_w.shape)],
        out_specs=pl.BlockSpec((None, tq, hw), lambda b, h, i: (b, i, h)),
        out_shape=jax.ShapeDtypeStruct((batch, seq, DIFF_WIDTH), BF16),
        scratch_shapes=[pltpu.VMEM((2 * tq, hw), BF16),
                        pltpu.VMEM((2 * tq, 1), F32),
                        pltpu.VMEM((2 * tq, 1), F32),
                        pltpu.VMEM((2 * tq, hw), F32)],
        compiler_params=_params("arbitrary", "arbitrary", "arbitrary"),
        name="diff_attn",
    )(lam_vecs, q, k, v, subln_w)
    return out.reshape(batch * seq, DIFF_WIDTH)


def _out_proj_kernel(*refs, n_in, has_bias):
    a_refs = refs[:n_in]
    w_ref = refs[n_in]
    b_ref = refs[n_in + 1] if has_bias else None
    nw_ref, x_ref, o_ref = refs[n_in + 1 + has_bias:]
    a = jnp.concatenate([r[...] for r in a_refs], axis=1) if n_in > 1 else a_refs[0][...]
    h = jnp.dot(a, w_ref[...], preferred_element_type=F32)
    if has_bias:
        h = h + b_ref[...]
    o_ref[...] = x_ref[...] + _rmsnorm(h, nw_ref[...], RMS_EPS)


def _out_proj(acts, w, bias, nw, x, tm=512):
    n = x.shape[0]
    row = lambda width: pl.BlockSpec((tm, width), lambda i: (i, 0))
    has_bias = bias is not None
    operands = list(acts) + [w] + ([bias] if has_bias else []) + [nw, x]
    in_specs = ([row(a.shape[1]) for a in acts] + [_resident(w.shape)]
                + ([_resident(bias.shape)] if has_bias else [])
                + [_resident(nw.shape), row(D_MODEL)])
    return pl.pallas_call(
        functools.partial(_out_proj_kernel, n_in=len(acts), has_bias=has_bias),
        grid=(n // tm,),
        in_specs=in_specs,
        out_specs=row(D_MODEL),
        out_shape=jax.ShapeDtypeStruct((n, D_MODEL), F32),
        compiler_params=_params("arbitrary"),
        name="out_proj",
    )(*operands)


def _mlp_kernel(x_ref, nw1_ref, w1_ref, w2_ref, nw2_ref, o_ref, *, ff_chunk):
    x = x_ref[...]
    hb = _rmsnorm(x, nw1_ref[...], RMS_EPS).astype(BF16)
    acc = None
    for c0 in range(0, D_FF, ff_chunk):
        t = jnp.dot(hb, w1_ref[:, c0:c0 + ff_chunk], preferred_element_type=F32)
        t = jnp.square(jnp.maximum(t, 0.0)).astype(BF16)
        part = jnp.dot(t, w2_ref[c0:c0 + ff_chunk, :], preferred_element_type=F32)
        acc = part if acc is None else acc + part
    o_ref[...] = x + _rmsnorm(acc, nw2_ref[...], RMS_EPS)


def _mlp(x, nw1, w1, w2, nw2, tm=512, ff_chunk=1024):
    n = x.shape[0]
    row = pl.BlockSpec((tm, D_MODEL), lambda i: (i, 0))
    return pl.pallas_call(
        functools.partial(_mlp_kernel, ff_chunk=ff_chunk),
        grid=(n // tm,),
        in_specs=[row, _resident(nw1.shape), _resident(w1.shape), _resident(w2.shape),
                  _resident(nw2.shape)],
        out_specs=row,
        out_shape=jax.ShapeDtypeStruct((n, D_MODEL), F32),
        compiler_params=_params("arbitrary"),
        name="mlp",
    )(x, nw1, w1, w2, nw2)


def _proj1_kernel(x_ref, nw_ref, w_ref, b_ref, cos_ref, sin_ref, q_ref, k_ref, v_ref):
    hb = _rmsnorm(x_ref[...], nw_ref[...], RMS_EPS).astype(BF16)

    def proj(c0, width):
        return (jnp.dot(hb, w_ref[:, c0:c0 + width], preferred_element_type=F32)
                + b_ref[:, c0:c0 + width])

    cos_t = cos_ref[...]
    sin_t = sin_ref[...]
    scale = HEAD_DIM ** -0.5
    nq = SWA_HEADS * HEAD_DIM
    for j in range(nq // LANES):
        qj = proj(j * LANES, LANES)
        q_ref[:, j * LANES:(j + 1) * LANES] = (_rope(qj, cos_t, sin_t) * scale).astype(BF16)

    def pair(t):
        swapped = pltpu.roll(t, HEAD_DIM, 1)
        low = lax.broadcasted_iota(jnp.int32, t.shape, 1) < HEAD_DIM
        return jnp.concatenate([jnp.where(low, t, swapped), jnp.where(low, swapped, t)], axis=1)

    k_ref[...] = pair(_rope(proj(nq, LANES), cos_t, sin_t)).astype(BF16)
    v_ref[...] = pair(proj(nq + LANES, LANES)).astype(BF16)


def _proj1(x, nw, w, b, cos_t, sin_t, tm=256):
    n = x.shape[0]
    row = lambda width: pl.BlockSpec((tm, width), lambda i: (i, 0))
    nq = SWA_HEADS * HEAD_DIM
    return pl.pallas_call(
        _proj1_kernel,
        grid=(n // tm,),
        in_specs=[row(D_MODEL), _resident(nw.shape), _resident(w.shape), _resident(b.shape),
                  row(LANES), row(LANES)],
        out_specs=[row(nq), row(2 * LANES), row(2 * LANES)],
        out_shape=[jax.ShapeDtypeStruct((n, nq), BF16),
                   jax.ShapeDtypeStruct((n, 2 * LANES), BF16),
                   jax.ShapeDtypeStruct((n, 2 * LANES), BF16)],
        compiler_params=_params("arbitrary"),
        name="proj1",
    )(x, nw, w, b, cos_t, sin_t)


def _swa_kernel(sink_ref, q_ref, kp_ref, kc_ref, vp_ref, vc_ref, o_ref):
    n = pl.program_id(1)
    w = WINDOW
    i = lax.broadcasted_iota(jnp.int32, (SWA_GROUP * w, 2 * w), 0) % w
    j = lax.broadcasted_iota(jnp.int32, (SWA_GROUP * w, 2 * w), 1)
    dist = i + w - j
    mask = (dist >= 0) & (dist < w) & ((n * w - w + j) >= 0)
    lane = lax.broadcasted_iota(jnp.int32, (w, LANES), 1)
    low = lane < HEAD_DIM
    pairs = SWA_GROUP // 2

    for kh in range(SWA_KV_HEADS):
        cols = slice(kh * LANES, (kh + 1) * LANES)
        k = jnp.concatenate([kp_ref[:, cols], kc_ref[:, cols]], axis=0)
        v = jnp.concatenate([vp_ref[:, cols], vc_ref[:, cols]], axis=0)
        rows = []
        sinks = []
        for p in range(pairs):
            c0 = (kh * pairs + p) * LANES
            q2 = q_ref[:, c0:c0 + LANES].astype(F32)
            rows.append(jnp.where(low, q2, 0.0).astype(BF16))
            rows.append(jnp.where(low, 0.0, q2).astype(BF16))
            for g in (2 * p, 2 * p + 1):
                sinks.append(jnp.full((w, 1), sink_ref[kh * SWA_GROUP + g], F32))
        qs = jnp.concatenate(rows, axis=0)
        sink = jnp.concatenate(sinks, axis=0)
        s = lax.dot_general(qs, k, (((1,), (1,)), ((), ())), preferred_element_type=F32)
        s = jnp.where(mask, s, -jnp.inf)
        m = jnp.maximum(jnp.max(s, axis=-1, keepdims=True), sink)
        e = jnp.exp(s - m)
        denom = jnp.sum(e, axis=-1, keepdims=True) + jnp.exp(sink - m)
        o = jnp.dot(e.astype(BF16), v, preferred_element_type=F32) / denom
        for p in range(pairs):
            c0 = (kh * pairs + p) * LANES
            even = o[(2 * p) * w:(2 * p + 1) * w]
            odd = o[(2 * p + 1) * w:(2 * p + 2) * w]
            o_ref[:, c0:c0 + LANES] = jnp.where(low, even, odd).astype(o_ref.dtype)


def _swa_attention(q, k, v, sinks, batch, seq):
    nq = SWA_HEADS * HEAD_DIM
    q = q.reshape(batch, seq, nq)
    k = k.reshape(batch, seq, 2 * LANES)
    v = v.reshape(batch, seq, 2 * LANES)
    w = WINDOW
    cur = pl.BlockSpec((None, w, 2 * LANES), lambda b, n: (b, n, 0))
    prev = pl.BlockSpec((None, w, 2 * LANES), lambda b, n: (b, jnp.maximum(n - 1, 0), 0))
    out = pl.pallas_call(
        _swa_kernel,
        grid=(batch, seq // w),
        in_specs=[pl.BlockSpec(memory_space=pltpu.SMEM),
                  pl.BlockSpec((None, w, nq), lambda b, n: (b, n, 0)),
                  prev, cur, prev, cur],
        out_specs=pl.BlockSpec((None, w, nq), lambda b, n: (b, n, 0)),
        out_shape=jax.ShapeDtypeStruct((batch, seq, nq), BF16),
        compiler_params=_params("arbitrary", "arbitrary"),
        name="swa_attn",
    )(sinks, q, k, k, v, v)
    return out.reshape(batch * seq, nq)


def kernel(x, positions, norm_pre_mix, norm_post_mix, norm_pre_mlp, norm_post_mlp, even_w_in, even_conv_w, even_lambda_q1, even_lambda_k1, even_lambda_q2, even_lambda_k2, even_subln_w, even_w_out, odd_w_qkv, odd_b_qkv, odd_sinks, odd_w_o, odd_b_o, mlp_w1, mlp_w2):
    batch, seq, d = x.shape
    n = batch * seq
    xf = x.reshape(n, d)
    cos_t, sin_t = _rope_tables(positions)
    row = lambda t: t.reshape(1, -1)

    lam_init = 0.8 - 0.6 * math.exp(-0.3 * 0)
    conv, q, k, v = _proj0(xf, row(norm_pre_mix[0]), even_w_in[0].astype(BF16), cos_t, sin_t,
                           even_conv_w[0], seq)
    lam_vecs = jnp.stack([even_lambda_q1[0], even_lambda_k1[0],
                          even_lambda_q2[0], even_lambda_k2[0]]).astype(F32)
    diff = _diff_attention(q, k, v, lam_vecs, row(even_subln_w[0]), lam_init, batch, seq)
    xf = _out_proj([conv, diff], even_w_out[0].astype(BF16), None, row(norm_post_mix[0]), xf)
    xf = _mlp(xf, row(norm_pre_mlp[0]), mlp_w1[0].astype(BF16), mlp_w2[0].astype(BF16),
              row(norm_post_mlp[0]))

    q, k, v = _proj1(xf, row(norm_pre_mix[1]), odd_w_qkv[0].astype(BF16), row(odd_b_qkv[0]),
                     cos_t, sin_t)
    attn = _swa_attention(q, k, v, odd_sinks[0], batch, seq)
    xf = _out_proj([attn], odd_w_o[0].astype(BF16), row(odd_b_o[0]), row(norm_post_mix[1]), xf)
    xf = _mlp(xf, row(norm_pre_mlp[1]), mlp_w1[1].astype(BF16), mlp_w2[1].astype(BF16),
              row(norm_post_mlp[1]))
    return xf.reshape(batch, seq, d)
```

```python
import functools
import math

import jax
import jax.numpy as jnp
from jax import lax
from jax.experimental import pallas as pl
from jax.experimental.pallas import tpu as pltpu

D_MODEL = 1024
HEAD_DIM = 64
ROT_HALF = 8
ROPE_THETA = 500000.0
RMS_EPS = 1e-6
CONV_WIDTH = 512
DIFF_HEADS = 4
DIFF_WIDTH = 512
DIFF_EPS = 1e-5
SWA_HEADS = 16
SWA_GROUP = 8
SWA_KV_HEADS = 2
WINDOW = 128
D_FF = 4096
LANES = 128
SUBLANES = 8
LOG2E = 1.4426950408889634
Q_SCALE = HEAD_DIM ** -0.5 * LOG2E

VMEM_LIMIT = 56 * 1024 * 1024

BF16 = jnp.bfloat16
F32 = jnp.float32
NT_DIMS = (((1,), (1,)), ((), ()))


def _params(*sem):
    return pltpu.CompilerParams(dimension_semantics=sem, vmem_limit_bytes=VMEM_LIMIT)


def _resident(shape):
    return pl.BlockSpec(shape, lambda *_: (0,) * len(shape), pipeline_mode=pl.Buffered(1))


def _rmsnorm(x, w, eps):
    return x * lax.rsqrt(jnp.mean(x * x, axis=-1, keepdims=True) + eps) * w


def _rope(x, cos_t, sin_t):
    lane = lax.broadcasted_iota(jnp.int32, x.shape, 1) % HEAD_DIM
    first = lane < ROT_HALF
    partner = jnp.where(first, pltpu.roll(x, LANES - ROT_HALF, 1), pltpu.roll(x, ROT_HALF, 1))
    return x * cos_t + partner * jnp.where(first, -sin_t, sin_t)


def _rope_blocks(x, cos_t, sin_t, scale=None):
    out = []
    for j in range(x.shape[1] // LANES):
        r = _rope(x[:, j * LANES:(j + 1) * LANES], cos_t, sin_t)
        out.append((r if scale is None else r * scale).astype(BF16))
    return jnp.concatenate(out, axis=1)


def _rope_table_kernel(invf_ref, pos_ref, cos_ref, sin_ref):
    pos = pos_ref[...].astype(F32)
    for f in range(ROT_HALF):
        ang = pos * invf_ref[f]
        cos_ref[f] = jnp.cos(ang)
        sin_ref[f] = jnp.sin(ang)


def _rope_tables(positions):
    n = positions.size
    inv_freq = ROPE_THETA ** (-jnp.arange(0, 2 * ROT_HALF, 2, dtype=F32) / (2 * ROT_HALF))
    pos2d = positions.reshape(n // LANES, LANES)
    cos_c, sin_c = pl.pallas_call(
        _rope_table_kernel,
        out_shape=[jax.ShapeDtypeStruct((ROT_HALF, n // LANES, LANES), F32)] * 2,
        in_specs=[pl.BlockSpec(memory_space=pltpu.SMEM),
                  pl.BlockSpec(memory_space=pltpu.VMEM)],
        out_specs=[pl.BlockSpec(memory_space=pltpu.VMEM)] * 2,
        name="rope_tables",
    )(inv_freq, pos2d)

    rotary = (jnp.arange(LANES) % HEAD_DIM) < 2 * ROT_HALF

    def expand(t, fill):
        t = t.reshape(ROT_HALF, n).T
        return jnp.where(rotary, jnp.tile(t, (1, LANES // ROT_HALF)), fill)

    return expand(cos_c, 1.0), expand(sin_c, 0.0)


def _proj0_kernel(x_ref, nw_ref, w_ref, cos_ref, sin_ref, cw_ref,
                  conv_ref, q_ref, k_ref, v_ref, carry_ref, *, tiles_per_seq):
    i = pl.program_id(0)
    tm = x_ref.shape[0]
    c = CONV_WIDTH
    hb = _rmsnorm(x_ref[...], nw_ref[...], RMS_EPS).astype(BF16)

    def proj(block):
        return jnp.dot(hb, w_ref[:, block * c:(block + 1) * c], preferred_element_type=F32)

    gb = proj(0)
    u = proj(1) * proj(2)

    @pl.when(i % tiles_per_seq == 0)
    def _():
        carry_ref[...] = jnp.zeros_like(carry_ref)

    prev = carry_ref[...]
    row = lax.broadcasted_iota(jnp.int32, u.shape, 0)
    u1 = jnp.where(row == 0, prev[SUBLANES - 1:SUBLANES], pltpu.roll(u, 1, 0))
    u2 = jnp.where(row == 0, prev[SUBLANES - 2:SUBLANES - 1],
                   jnp.where(row == 1, prev[SUBLANES - 1:SUBLANES], pltpu.roll(u, 2, 0)))
    cw = cw_ref[...]
    conv = cw[0:1] * u2 + cw[1:2] * u1 + cw[2:3] * u
    conv_ref[...] = (gb * conv).astype(BF16)
    carry_ref[...] = u[tm - SUBLANES:, :]

    cos_t = cos_ref[...]
    sin_t = sin_ref[...]
    q_ref[...] = _rope_blocks(proj(3), cos_t, sin_t, Q_SCALE)
    k_ref[...] = _rope_blocks(proj(4), cos_t, sin_t)
    v_ref[...] = proj(5).astype(BF16)


def _proj0(x, nw, w, cos_t, sin_t, conv_w, seq, tm=512):
    n = x.shape[0]
    row = lambda width: pl.BlockSpec((tm, width), lambda i: (i, 0))
    out = jax.ShapeDtypeStruct((n, CONV_WIDTH), BF16)
    return pl.pallas_call(
        functools.partial(_proj0_kernel, tiles_per_seq=seq // tm),
        grid=(n // tm,),
        in_specs=[row(D_MODEL), _resident((1, D_MODEL)), _resident(w.shape),
                  row(LANES), row(LANES), _resident(conv_w.shape)],
        out_specs=[row(CONV_WIDTH)] * 4,
        out_shape=[out] * 4,
        scratch_shapes=[pltpu.VMEM((SUBLANES, CONV_WIDTH), F32)],
        compiler_params=_params("arbitrary"),
        name="proj0",
    )(x, nw, w, cos_t, sin_t, conv_w)


def _diff_attn_kernel(lam_ref, q_ref, k_ref, v_ref, sw_ref, o_ref,
                      qs_ref, va_ref, m_ref, acc_ref, *, lam_init, tk, chain_rows):
    qi = pl.program_id(2)
    tq = q_ref.shape[0]
    hw = q_ref.shape[1]

    @pl.when(qi == 0)
    def _():
        va_ref[:, 0:hw] = v_ref[...]
        va_ref[:, hw:] = jnp.ones((va_ref.shape[0], hw), BF16)

    q = q_ref[...].astype(F32)
    lane = lax.broadcasted_iota(jnp.int32, q.shape, 1)
    qs_ref[0:tq, :] = jnp.where(lane < HEAD_DIM, q, 0.0).astype(BF16)
    qs_ref[tq:, :] = jnp.where(lane >= HEAD_DIM, q, 0.0).astype(BF16)
    m_ref[...] = jnp.full_like(m_ref, -jnp.inf)
    acc_ref[...] = jnp.zeros_like(acc_ref)

    def step(start, q_off):
        k = k_ref[pl.ds(start, tk), :]
        va = va_ref[pl.ds(start, tk), :]
        chains = [slice(r0, r0 + chain_rows) for r0 in range(0, 2 * tq, chain_rows)]
        scores = [lax.dot_general(qs_ref[rows, :], k, NT_DIMS, preferred_element_type=F32)
                  for rows in chains]
        for rows, s in zip(chains, scores):
            if q_off is not None:
                r = lax.broadcasted_iota(jnp.int32, s.shape, 0) + rows.start % tq
                col = lax.broadcasted_iota(jnp.int32, s.shape, 1)
                s = jnp.where(col <= r + q_off, s, -jnp.inf)
            m_prev = m_ref[rows, :]
            m_new = jnp.maximum(m_prev, jnp.max(s, axis=-1, keepdims=True))
            alpha = jnp.exp2(m_prev - m_new)
            p = jnp.exp2(s - jnp.concatenate([m_new] * (tk // LANES), axis=1)).astype(BF16)
            pv = jnp.dot(p, va, preferred_element_type=F32)
            acc_ref[rows, :] = jnp.concatenate([alpha, alpha], axis=1) * acc_ref[rows, :] + pv
            m_ref[rows, :] = m_new

    n_full = (qi * tq) // tk

    def body(kb, carry):
        step(pl.multiple_of(kb * tk, tk), None)
        return carry

    lax.fori_loop(0, n_full, body, 0)
    step(pl.multiple_of(n_full * tk, tk), qi * tq - n_full * tk)

    lam_v = lam_ref[...]
    lam = (jnp.exp(jnp.sum(lam_v[0:1] * lam_v[1:2], keepdims=True))
           - jnp.exp(jnp.sum(lam_v[2:3] * lam_v[3:4], keepdims=True)) + lam_init)
    acc = acc_ref[...]
    o = acc[:, 0:hw] / acc[:, hw:]
    o = o[0:tq] - lam * o[tq:]
    o = _rmsnorm(o, sw_ref[...], DIFF_EPS) * (1.0 - lam_init)
    o_ref[...] = o.astype(o_ref.dtype)


def _diff_attention(q, k, v, lam_vecs, subln_w, lam_init, batch, seq, tq=512, tk=1024,
                    chain_rows=256):
    q, k, v = (t.reshape(batch, seq, DIFF_WIDTH) for t in (q, k, v))
    hw = 2 * HEAD_DIM
    out = pl.pallas_call(
        functools.partial(_diff_attn_kernel, lam_init=lam_init, tk=tk, chain_rows=chain_rows),
        grid=(batch, DIFF_HEADS, seq // tq),
        in_specs=[_resident(lam_vecs.shape),
                  pl.BlockSpec((None, tq, hw), lambda b, h, i: (b, i, h)),
                  pl.BlockSpec((None, seq, hw), lambda b, h, i: (b, 0, h)),
                  pl.BlockSpec((None, seq, hw), lambda b, h, i: (b, 0, h)),
                  _resident(subln_w.shape)],
        out_specs=pl.BlockSpec((None, tq, hw), lambda b, h, i: (b, i, h)),
        out_shape=jax.ShapeDtypeStruct((batch, seq, DIFF_WIDTH), BF16),
        scratch_shapes=[pltpu.VMEM((2 * tq, hw), BF16),
                        pltpu.VMEM((seq, 2 * hw), BF16),
                        pltpu.VMEM((2 * tq, hw), F32),
                        pltpu.VMEM((2 * tq, 2 * hw), F32)],
        compiler_params=_params("arbitrary", "arbitrary", "arbitrary"),
        name="diff_attn",
    )(lam_vecs, q, k, v, subln_w)
    return out.reshape(batch * seq, DIFF_WIDTH)


def _mix_mlp_kernel(*refs, n_in, has_bias, ff_chunk):
    a_refs = refs[:n_in]
    wo_ref = refs[n_in]
    b_ref = refs[n_in + 1] if has_bias else None
    nwo_ref, x_ref, nw1_ref, w1_ref, w2_ref, nw2_ref, o_ref = refs[n_in + 1 + has_bias:]
    a = jnp.concatenate([r[...] for r in a_refs], axis=1) if n_in > 1 else a_refs[0][...]
    h = jnp.dot(a, wo_ref[...], preferred_element_type=F32)
    if has_bias:
        h = h + b_ref[...]
    x1 = x_ref[...] + _rmsnorm(h, nwo_ref[...], RMS_EPS)

    hb = _rmsnorm(x1, nw1_ref[...], RMS_EPS).astype(BF16)
    acc = None
    for c0 in range(0, D_FF, ff_chunk):
        t = jnp.dot(hb, w1_ref[:, c0:c0 + ff_chunk], preferred_element_type=F32)
        t = jnp.square(jnp.maximum(t, 0.0)).astype(BF16)
        part = jnp.dot(t, w2_ref[c0:c0 + ff_chunk, :], preferred_element_type=F32)
        acc = part if acc is None else acc + part
    o_ref[...] = x1 + _rmsnorm(acc, nw2_ref[...], RMS_EPS)


def _mix_mlp(acts, wo, bias, nwo, x, nw1, w1, w2, nw2, tm=512, ff_chunk=1024):
    n = x.shape[0]
    row = lambda width: pl.BlockSpec((tm, width), lambda i: (i, 0))
    has_bias = bias is not None
    operands = (list(acts) + [wo] + ([bias] if has_bias else [])
                + [nwo, x, nw1, w1, w2, nw2])
    in_specs = ([row(a.shape[1]) for a in acts] + [_resident(wo.shape)]
                + ([_resident(bias.shape)] if has_bias else [])
                + [_resident(nwo.shape), row(D_MODEL), _resident(nw1.shape),
                   _resident(w1.shape), _resident(w2.shape), _resident(nw2.shape)])
    return pl.pallas_call(
        functools.partial(_mix_mlp_kernel, n_in=len(acts), has_bias=has_bias, ff_chunk=ff_chunk),
        grid=(n // tm,),
        in_specs=in_specs,
        out_specs=row(D_MODEL),
        out_shape=jax.ShapeDtypeStruct((n, D_MODEL), F32),
        compiler_params=_params("arbitrary"),
        name="mix_mlp",
    )(*operands)


def _proj1_kernel(x_ref, nw_ref, w_ref, b_ref, cos_ref, sin_ref, q_ref, k_ref, v_ref):
    hb = _rmsnorm(x_ref[...], nw_ref[...], RMS_EPS).astype(BF16)

    def proj(c0, width):
        return (jnp.dot(hb, w_ref[:, c0:c0 + width], preferred_element_type=F32)
                + b_ref[:, c0:c0 + width])

    cos_t = cos_ref[...]
    sin_t = sin_ref[...]
    nq = SWA_HEADS * HEAD_DIM
    half = nq // 2
    q_ref[:, 0:half] = _rope_blocks(proj(0, half), cos_t, sin_t, Q_SCALE)
    q_ref[:, half:] = _rope_blocks(proj(half, half), cos_t, sin_t, Q_SCALE)

    def pair(t):
        swapped = pltpu.roll(t, HEAD_DIM, 1)
        low = lax.broadcasted_iota(jnp.int32, t.shape, 1) < HEAD_DIM
        return jnp.concatenate([jnp.where(low, t, swapped), jnp.where(low, swapped, t)], axis=1)

    kv = proj(nq, 2 * LANES)
    k_ref[...] = pair(_rope(kv[:, 0:LANES], cos_t, sin_t)).astype(BF16)
    v_ref[...] = pair(kv[:, LANES:]).astype(BF16)


def _proj1(x, nw, w, b, cos_t, sin_t, tm=512):
    n = x.shape[0]
    row = lambda width: pl.BlockSpec((tm, width), lambda i: (i, 0))
    nq = SWA_HEADS * HEAD_DIM
    return pl.pallas_call(
        _proj1_kernel,
        grid=(n // tm,),
        in_specs=[row(D_MODEL), _resident(nw.shape), _resident(w.shape), _resident(b.shape),
                  row(LANES), row(LANES)],
        out_specs=[row(nq), row(2 * LANES), row(2 * LANES)],
        out_shape=[jax.ShapeDtypeStruct((n, nq), BF16),
                   jax.ShapeDtypeStruct((n, 2 * LANES), BF16),
                   jax.ShapeDtypeStruct((n, 2 * LANES), BF16)],
        compiler_params=_params("arbitrary"),
        name="proj1",
    )(x, nw, w, b, cos_t, sin_t)


def _swa_kernel(sink_ref, q_ref, kp_ref, kc_ref, vp_ref, vc_ref, o_ref):
    n = pl.program_id(1)
    w = WINDOW
    i = lax.broadcasted_iota(jnp.int32, (w, 2 * w), 0)
    j = lax.broadcasted_iota(jnp.int32, (w, 2 * w), 1)
    dist = i + w - j
    valid = (dist >= 0) & (dist < w) & ((n * w - w + j) >= 0)
    bias = jnp.where(valid, 0.0, -jnp.inf)
    low = lax.broadcasted_iota(jnp.int32, (w, LANES), 1) < HEAD_DIM
    ones = jnp.ones((2 * w, LANES), BF16)
    pairs = SWA_GROUP // 2

    for kh in range(SWA_KV_HEADS):
        cols = slice(kh * LANES, (kh + 1) * LANES)
        k = jnp.concatenate([kp_ref[:, cols], kc_ref[:, cols]], axis=0)
        va = jnp.concatenate(
            [jnp.concatenate([vp_ref[:, cols], vc_ref[:, cols]], axis=0), ones], axis=1)
        rows = []
        for p in range(pairs):
            c0 = (kh * pairs + p) * LANES
            q2 = q_ref[:, c0:c0 + LANES].astype(F32)
            rows.append(jnp.where(low, q2, 0.0).astype(BF16))
            rows.append(jnp.where(low, 0.0, q2).astype(BF16))
        qs = jnp.concatenate(rows, axis=0)
        s = lax.dot_general(qs, k, NT_DIMS, preferred_element_type=F32)

        probs = []
        sink_terms = []
        for g in range(SWA_GROUP):
            sink = sink_ref[kh * SWA_GROUP + g] * LOG2E
            sg = s[g * w:(g + 1) * w] + bias
            m = jnp.maximum(jnp.max(sg, axis=-1, keepdims=True), sink)
            m = jnp.broadcast_to(m, (w, LANES))
            probs.append(jnp.exp2(sg - jnp.concatenate([m, m], axis=1)).astype(BF16))
            sink_terms.append(jnp.exp2(sink - m))
        pv = jnp.dot(jnp.concatenate(probs, axis=0), va, preferred_element_type=F32)

        outs = []
        for g in range(SWA_GROUP):
            blk = pv[g * w:(g + 1) * w]
            outs.append(blk[:, 0:LANES] / (blk[:, LANES:] + sink_terms[g]))
        for p in range(pairs):
            c0 = (kh * pairs + p) * LANES
            o_ref[:, c0:c0 + LANES] = jnp.where(low, outs[2 * p], outs[2 * p + 1]).astype(o_ref.dtype)


def _swa_attention(q, k, v, sinks, batch, seq):
    nq = SWA_HEADS * HEAD_DIM
    q = q.reshape(batch, seq, nq)
    k = k.reshape(batch, seq, 2 * LANES)
    v = v.reshape(batch, seq, 2 * LANES)
    w = WINDOW
    cur = pl.BlockSpec((None, w, 2 * LANES), lambda b, n: (b, n, 0))
    prev = pl.BlockSpec((None, w, 2 * LANES), lambda b, n: (b, jnp.maximum(n - 1, 0), 0))
    out = pl.pallas_call(
        _swa_kernel,
        grid=(batch, seq // w),
        in_specs=[pl.BlockSpec(memory_space=pltpu.SMEM),
                  pl.BlockSpec((None, w, nq), lambda b, n: (b, n, 0)),
                  prev, cur, prev, cur],
        out_specs=pl.BlockSpec((None, w, nq), lambda b, n: (b, n, 0)),
        out_shape=jax.ShapeDtypeStruct((batch, seq, nq), BF16),
        compiler_params=_params("arbitrary", "arbitrary"),
        name="swa_attn",
    )(sinks, q, k, k, v, v)
    return out.reshape(batch * seq, nq)


def kernel(x, positions, norm_pre_mix, norm_post_mix, norm_pre_mlp, norm_post_mlp, even_w_in, even_conv_w, even_lambda_q1, even_lambda_k1, even_lambda_q2, even_lambda_k2, even_subln_w, even_w_out, odd_w_qkv, odd_b_qkv, odd_sinks, odd_w_o, odd_b_o, mlp_w1, mlp_w2):
    batch, seq, d = x.shape
    n = batch * seq
    xf = x.reshape(n, d)
    cos_t, sin_t = _rope_tables(positions)
    row = lambda t: t.reshape(1, -1)

    lam_init = 0.8 - 0.6 * math.exp(-0.3 * 0)
    conv, q, k, v = _proj0(xf, row(norm_pre_mix[0]), even_w_in[0].astype(BF16), cos_t, sin_t,
                           even_conv_w[0], seq)
    lam_vecs = jnp.stack([even_lambda_q1[0], even_lambda_k1[0],
                          even_lambda_q2[0], even_lambda_k2[0]]).astype(F32)
    diff = _diff_attention(q, k, v, lam_vecs, row(even_subln_w[0]), lam_init, batch, seq)
    xf = _mix_mlp([conv, diff], even_w_out[0].astype(BF16), None, row(norm_post_mix[0]), xf,
                  row(norm_pre_mlp[0]), mlp_w1[0].astype(BF16), mlp_w2[0].astype(BF16),
                  row(norm_post_mlp[0]))

    q, k, v = _proj1(xf, row(norm_pre_mix[1]), odd_w_qkv[0].astype(BF16), row(odd_b_qkv[0]),
                     cos_t, sin_t)
    attn = _swa_attention(q, k, v, odd_sinks[0], batch, seq)
    xf = _mix_mlp([attn], odd_w_o[0].astype(BF16), row(odd_b_o[0]), row(norm_post_mix[1]), xf,
                  row(norm_pre_mlp[1]), mlp_w1[1].astype(BF16), mlp_w2[1].astype(BF16),
                  row(norm_post_mlp[1]))
    return xf.reshape(batch, seq, d)
```

```python
import functools
import math

import jax
import jax.numpy as jnp
from jax import lax
from jax.experimental import pallas as pl
from jax.experimental.pallas import tpu as pltpu

D_MODEL = 1024
HEAD_DIM = 64
ROT_HALF = 8
ROPE_THETA = 500000.0
RMS_EPS = 1e-6
CONV_WIDTH = 512
DIFF_HEADS = 4
DIFF_WIDTH = 512
DIFF_EPS = 1e-5
SWA_HEADS = 16
SWA_GROUP = 8
SWA_KV_HEADS = 2
WINDOW = 128
D_FF = 4096
LANES = 128
SUBLANES = 8
LOG2E = 1.4426950408889634
Q_SCALE = HEAD_DIM ** -0.5 * LOG2E

VMEM_LIMIT = 56 * 1024 * 1024

BF16 = jnp.bfloat16
F32 = jnp.float32
NT_DIMS = (((1,), (1,)), ((), ()))


def _params(*sem):
    return pltpu.CompilerParams(dimension_semantics=sem, vmem_limit_bytes=VMEM_LIMIT)


def _resident(shape):
    return pl.BlockSpec(shape, lambda *_: (0,) * len(shape), pipeline_mode=pl.Buffered(1))


def _rmsnorm(x, w, eps):
    return x * lax.rsqrt(jnp.mean(x * x, axis=-1, keepdims=True) + eps) * w


def _rope(x, cos_t, sin_t):
    lane = lax.broadcasted_iota(jnp.int32, x.shape, 1) % HEAD_DIM
    first = lane < ROT_HALF
    partner = jnp.where(first, pltpu.roll(x, LANES - ROT_HALF, 1), pltpu.roll(x, ROT_HALF, 1))
    return x * cos_t + partner * jnp.where(first, -sin_t, sin_t)


def _rope_blocks(x, cos_t, sin_t, scale=None):
    out = []
    for j in range(x.shape[1] // LANES):
        r = _rope(x[:, j * LANES:(j + 1) * LANES], cos_t, sin_t)
        out.append((r if scale is None else r * scale).astype(BF16))
    return jnp.concatenate(out, axis=1)


def _rope_table_kernel(invf_ref, pos_ref, cos_ref, sin_ref):
    pos = pos_ref[...].astype(F32)
    for f in range(ROT_HALF):
        ang = pos * invf_ref[f]
        cos_ref[f] = jnp.cos(ang)
        sin_ref[f] = jnp.sin(ang)


def _rope_tables(positions):
    n = positions.size
    inv_freq = ROPE_THETA ** (-jnp.arange(0, 2 * ROT_HALF, 2, dtype=F32) / (2 * ROT_HALF))
    pos2d = positions.reshape(n // LANES, LANES)
    cos_c, sin_c = pl.pallas_call(
        _rope_table_kernel,
        out_shape=[jax.ShapeDtypeStruct((ROT_HALF, n // LANES, LANES), F32)] * 2,
        in_specs=[pl.BlockSpec(memory_space=pltpu.SMEM),
                  pl.BlockSpec(memory_space=pltpu.VMEM)],
        out_specs=[pl.BlockSpec(memory_space=pltpu.VMEM)] * 2,
        name="rope_tables",
    )(inv_freq, pos2d)

    rotary = (jnp.arange(LANES) % HEAD_DIM) < 2 * ROT_HALF

    def expand(t, fill):
        t = t.reshape(ROT_HALF, n).T
        return jnp.where(rotary, jnp.tile(t, (1, LANES // ROT_HALF)), fill)

    return expand(cos_c, 1.0), expand(sin_c, 0.0)


def _proj0_kernel(x_ref, nw_ref, w_ref, cos_ref, sin_ref, cw_ref,
                  conv_ref, q_ref, k_ref, v_ref, carry_ref, *, tiles_per_seq):
    i = pl.program_id(0)
    tm = x_ref.shape[0]
    c = CONV_WIDTH
    hb = _rmsnorm(x_ref[...], nw_ref[...], RMS_EPS).astype(BF16)

    def proj(block):
        return jnp.dot(hb, w_ref[:, block * c:(block + 1) * c], preferred_element_type=F32)

    gb = proj(0)
    u = proj(1) * proj(2)

    @pl.when(i % tiles_per_seq == 0)
    def _():
        carry_ref[...] = jnp.zeros_like(carry_ref)

    prev = carry_ref[...]
    row = lax.broadcasted_iota(jnp.int32, u.shape, 0)
    u1 = jnp.where(row == 0, prev[SUBLANES - 1:SUBLANES], pltpu.roll(u, 1, 0))
    u2 = jnp.where(row == 0, prev[SUBLANES - 2:SUBLANES - 1],
                   jnp.where(row == 1, prev[SUBLANES - 1:SUBLANES], pltpu.roll(u, 2, 0)))
    cw = cw_ref[...]
    conv = cw[0:1] * u2 + cw[1:2] * u1 + cw[2:3] * u
    conv_ref[...] = (gb * conv).astype(BF16)
    carry_ref[...] = u[tm - SUBLANES:, :]

    cos_t = cos_ref[...]
    sin_t = sin_ref[...]
    q_ref[...] = _rope_blocks(proj(3), cos_t, sin_t, Q_SCALE)
    k_ref[...] = _rope_blocks(proj(4), cos_t, sin_t)
    v_ref[...] = proj(5).astype(BF16)


def _proj0(x, nw, w, cos_t, sin_t, conv_w, seq, tm=512):
    n = x.shape[0]
    row = lambda width: pl.BlockSpec((tm, width), lambda i: (i, 0))
    out = jax.ShapeDtypeStruct((n, CONV_WIDTH), BF16)
    return pl.pallas_call(
        functools.partial(_proj0_kernel, tiles_per_seq=seq // tm),
        grid=(n // tm,),
        in_specs=[row(D_MODEL), _resident((1, D_MODEL)), _resident(w.shape),
                  row(LANES), row(LANES), _resident(conv_w.shape)],
        out_specs=[row(CONV_WIDTH)] * 4,
        out_shape=[out] * 4,
        scratch_shapes=[pltpu.VMEM((SUBLANES, CONV_WIDTH), F32)],
        compiler_params=_params("arbitrary"),
        name="proj0",
    )(x, nw, w, cos_t, sin_t, conv_w)


def _diff_attn_kernel(lam_ref, q_ref, k_ref, v_ref, sw_ref, o_ref,
                      qs_ref, va_ref, m_ref, acc_ref, *, lam_init, chain_rows):
    qi = pl.program_id(2)
    tq = q_ref.shape[0]
    hw = q_ref.shape[1]

    @pl.when(qi == 0)
    def _():
        va_ref[:, 0:hw] = v_ref[...]
        va_ref[:, hw:] = jnp.ones((va_ref.shape[0], hw), BF16)

    q = q_ref[...].astype(F32)
    lane = lax.broadcasted_iota(jnp.int32, q.shape, 1)
    qs_ref[0:tq, :] = jnp.where(lane < HEAD_DIM, q, 0.0).astype(BF16)
    qs_ref[tq:, :] = jnp.where(lane >= HEAD_DIM, q, 0.0).astype(BF16)

    chains = [slice(r0, r0 + chain_rows) for r0 in range(0, 2 * tq, chain_rows)]
    cr = chain_rows
    tri = (lax.broadcasted_iota(jnp.int32, (cr, cr), 1)
           <= lax.broadcasted_iota(jnp.int32, (cr, cr), 0))

    def step(start, diagonal, first):
        widths = [(rows.start % tq) + cr if diagonal else tq for rows in chains]
        scores = [lax.dot_general(qs_ref[rows, :], k_ref[pl.ds(start, w), :], NT_DIMS,
                                  preferred_element_type=F32)
                  for rows, w in zip(chains, widths)]
        for rows, w, s in zip(chains, widths, scores):
            if diagonal:
                last = jnp.where(tri, s[:, w - cr:], -jnp.inf)
                s = last if w == cr else jnp.concatenate([s[:, :w - cr], last], axis=1)
            m_new = jnp.max(s, axis=-1, keepdims=True)
            if first:
                m_new = jnp.broadcast_to(m_new, (cr, LANES))
            else:
                m_prev = m_ref[rows, :]
                m_new = jnp.maximum(m_prev, m_new)
            p = jnp.exp2(s - jnp.concatenate([m_new] * (w // LANES), axis=1)).astype(BF16)
            pv = jnp.dot(p, va_ref[pl.ds(start, w), :], preferred_element_type=F32)
            if not first:
                alpha = jnp.exp2(m_prev - m_new)
                pv = jnp.concatenate([alpha, alpha], axis=1) * acc_ref[rows, :] + pv
            acc_ref[rows, :] = pv
            m_ref[rows, :] = m_new

    @pl.when(qi == 0)
    def _():
        step(0, True, True)

    @pl.when(qi > 0)
    def _():
        step(0, False, True)

        def body(kb, carry):
            step(pl.multiple_of(kb * tq, tq), False, False)
            return carry

        lax.fori_loop(1, qi, body, 0)
        step(pl.multiple_of(qi * tq, tq), True, False)

    lam_v = lam_ref[...]
    lam = (jnp.exp(jnp.sum(lam_v[0:1] * lam_v[1:2], keepdims=True))
           - jnp.exp(jnp.sum(lam_v[2:3] * lam_v[3:4], keepdims=True)) + lam_init)
    acc = acc_ref[...]
    o = acc[:, 0:hw] / acc[:, hw:]
    o = o[0:tq] - lam * o[tq:]
    o = _rmsnorm(o, sw_ref[...], DIFF_EPS) * (1.0 - lam_init)
    o_ref[...] = o.astype(o_ref.dtype)


def _diff_attention(q, k, v, lam_vecs, subln_w, lam_init, batch, seq, tq=1024,
                    chain_rows=256):
    q, k, v = (t.reshape(batch, seq, DIFF_WIDTH) for t in (q, k, v))
    hw = 2 * HEAD_DIM
    out = pl.pallas_call(
        functools.partial(_diff_attn_kernel, lam_init=lam_init, chain_rows=chain_rows),
        grid=(batch, DIFF_HEADS, seq // tq),
        in_specs=[_resident(lam_vecs.shape),
                  pl.BlockSpec((None, tq, hw), lambda b, h, i: (b, i, h)),
                  pl.BlockSpec((None, seq, hw), lambda b, h, i: (b, 0, h)),
                  pl.BlockSpec((None, seq, hw), lambda b, h, i: (b, 0, h)),
                  _resident(subln_w.shape)],
        out_specs=pl.BlockSpec((None, tq, hw), lambda b, h, i: (b, i, h)),
        out_shape=jax.ShapeDtypeStruct((batch, seq, DIFF_WIDTH), BF16),
        scratch_shapes=[pltpu.VMEM((2 * tq, hw), BF16),
                        pltpu.VMEM((seq, 2 * hw), BF16),
                        pltpu.VMEM((2 * tq, hw), F32),
                        pltpu.VMEM((2 * tq, 2 * hw), F32)],
        compiler_params=_params("arbitrary", "arbitrary", "arbitrary"),
        name="diff_attn",
    )(lam_vecs, q, k, v, subln_w)
    return out.reshape(batch * seq, DIFF_WIDTH)


def _mix_mlp_kernel(*refs, n_in, has_bias, ff_chunk):
    a_refs = refs[:n_in]
    wo_ref = refs[n_in]
    b_ref = refs[n_in + 1] if has_bias else None
    nwo_ref, x_ref, nw1_ref, w1_ref, w2_ref, nw2_ref, o_ref = refs[n_in + 1 + has_bias:]
    a = jnp.concatenate([r[...] for r in a_refs], axis=1) if n_in > 1 else a_refs[0][...]
    h = jnp.dot(a, wo_ref[...], preferred_element_type=F32)
    if has_bias:
        h = h + b_ref[...]
    x1 = x_ref[...] + _rmsnorm(h, nwo_ref[...], RMS_EPS)

    hb = _rmsnorm(x1, nw1_ref[...], RMS_EPS).astype(BF16)
    acc = None
    for c0 in range(0, D_FF, ff_chunk):
        t = jnp.dot(hb, w1_ref[:, c0:c0 + ff_chunk], preferred_element_type=F32)
        t = jnp.square(jnp.maximum(t, 0.0)).astype(BF16)
        part = jnp.dot(t, w2_ref[c0:c0 + ff_chunk, :], preferred_element_type=F32)
        acc = part if acc is None else acc + part
    o_ref[...] = x1 + _rmsnorm(acc, nw2_ref[...], RMS_EPS)


def _mix_mlp(acts, wo, bias, nwo, x, nw1, w1, w2, nw2, layer, tm=512, ff_chunk=1024):
    n = x.shape[0]
    row = lambda width: pl.BlockSpec((tm, width), lambda i: (i, 0))
    stacked = lambda w: pl.BlockSpec((None,) + w.shape[1:], lambda i: (layer, 0, 0),
                                     pipeline_mode=pl.Buffered(1))
    has_bias = bias is not None
    operands = (list(acts) + [wo] + ([bias] if has_bias else [])
                + [nwo, x, nw1, w1, w2, nw2])
    in_specs = ([row(a.shape[1]) for a in acts] + [_resident(wo.shape)]
                + ([_resident(bias.shape)] if has_bias else [])
                + [_resident(nwo.shape), row(D_MODEL), _resident(nw1.shape),
                   stacked(w1), stacked(w2), _resident(nw2.shape)])
    return pl.pallas_call(
        functools.partial(_mix_mlp_kernel, n_in=len(acts), has_bias=has_bias, ff_chunk=ff_chunk),
        grid=(n // tm,),
        in_specs=in_specs,
        out_specs=row(D_MODEL),
        out_shape=jax.ShapeDtypeStruct((n, D_MODEL), F32),
        compiler_params=_params("arbitrary"),
        name="mix_mlp",
    )(*operands)


def _proj1_kernel(x_ref, nw_ref, w_ref, b_ref, cos_ref, sin_ref, q_ref, k_ref, v_ref):
    hb = _rmsnorm(x_ref[...], nw_ref[...], RMS_EPS).astype(BF16)

    def proj(c0, width):
        return (jnp.dot(hb, w_ref[:, c0:c0 + width], preferred_element_type=F32)
                + b_ref[:, c0:c0 + width])

    cos_t = cos_ref[...]
    sin_t = sin_ref[...]
    nq = SWA_HEADS * HEAD_DIM
    half = nq // 2
    q_ref[:, 0:half] = _rope_blocks(proj(0, half), cos_t, sin_t, Q_SCALE)
    q_ref[:, half:] = _rope_blocks(proj(half, half), cos_t, sin_t, Q_SCALE)

    def pair(t):
        swapped = pltpu.roll(t, HEAD_DIM, 1)
        low = lax.broadcasted_iota(jnp.int32, t.shape, 1) < HEAD_DIM
        return jnp.concatenate([jnp.where(low, t, swapped), jnp.where(low, swapped, t)], axis=1)

    kv = proj(nq, 2 * LANES)
    k_ref[...] = pair(_rope(kv[:, 0:LANES], cos_t, sin_t)).astype(BF16)
    v_ref[...] = pair(kv[:, LANES:]).astype(BF16)


def _proj1(x, nw, w, b, cos_t, sin_t, tm=512):
    n = x.shape[0]
    row = lambda width: pl.BlockSpec((tm, width), lambda i: (i, 0))
    nq = SWA_HEADS * HEAD_DIM
    return pl.pallas_call(
        _proj1_kernel,
        grid=(n // tm,),
        in_specs=[row(D_MODEL), _resident(nw.shape), _resident(w.shape), _resident(b.shape),
                  row(LANES), row(LANES)],
        out_specs=[row(nq), row(2 * LANES), row(2 * LANES)],
        out_shape=[jax.ShapeDtypeStruct((n, nq), BF16),
                   jax.ShapeDtypeStruct((n, 2 * LANES), BF16),
                   jax.ShapeDtypeStruct((n, 2 * LANES), BF16)],
        compiler_params=_params("arbitrary"),
        name="proj1",
    )(x, nw, w, b, cos_t, sin_t)


def _swa_kernel(sink_ref, q_ref, kp_ref, kc_ref, vp_ref, vc_ref, o_ref):
    n = pl.program_id(1)
    w = WINDOW
    i = lax.broadcasted_iota(jnp.int32, (w, 2 * w), 0)
    j = lax.broadcasted_iota(jnp.int32, (w, 2 * w), 1)
    dist = i + w - j
    valid = (dist >= 0) & (dist < w) & ((n * w - w + j) >= 0)
    bias = jnp.where(valid, 0.0, -jnp.inf)
    low = lax.broadcasted_iota(jnp.int32, (w, LANES), 1) < HEAD_DIM
    ones = jnp.ones((2 * w, LANES), BF16)
    pairs = SWA_GROUP // 2

    for kh in range(SWA_KV_HEADS):
        cols = slice(kh * LANES, (kh + 1) * LANES)
        k = jnp.concatenate([kp_ref[:, cols], kc_ref[:, cols]], axis=0)
        va = jnp.concatenate(
            [jnp.concatenate([vp_ref[:, cols], vc_ref[:, cols]], axis=0), ones], axis=1)
        rows = []
        for p in range(pairs):
            c0 = (kh * pairs + p) * LANES
            q2 = q_ref[:, c0:c0 + LANES].astype(F32)
            rows.append(jnp.where(low, q2, 0.0).astype(BF16))
            rows.append(jnp.where(low, 0.0, q2).astype(BF16))
        qs = jnp.concatenate(rows, axis=0)
        s = lax.dot_general(qs, k, NT_DIMS, preferred_element_type=F32)

        probs = []
        sink_terms = []
        for g in range(SWA_GROUP):
            sink = sink_ref[kh * SWA_GROUP + g] * LOG2E
            sg = s[g * w:(g + 1) * w] + bias
            m = jnp.maximum(jnp.max(sg, axis=-1, keepdims=True), sink)
            m = jnp.broadcast_to(m, (w, LANES))
            probs.append(jnp.exp2(sg - jnp.concatenate([m, m], axis=1)).astype(BF16))
            sink_terms.append(jnp.exp2(sink - m))
        pv = jnp.dot(jnp.concatenate(probs, axis=0), va, preferred_element_type=F32)

        outs = []
        for g in range(SWA_GROUP):
            blk = pv[g * w:(g + 1) * w]
            outs.append(blk[:, 0:LANES] / (blk[:, LANES:] + sink_terms[g]))
        for p in range(pairs):
            c0 = (kh * pairs + p) * LANES
            o_ref[:, c0:c0 + LANES] = jnp.where(low, outs[2 * p], outs[2 * p + 1]).astype(o_ref.dtype)


def _swa_attention(q, k, v, sinks, batch, seq):
    nq = SWA_HEADS * HEAD_DIM
    q = q.reshape(batch, seq, nq)
    k = k.reshape(batch, seq, 2 * LANES)
    v = v.reshape(batch, seq, 2 * LANES)
    w = WINDOW
    cur = pl.BlockSpec((None, w, 2 * LANES), lambda b, n: (b, n, 0))
    prev = pl.BlockSpec((None, w, 2 * LANES), lambda b, n: (b, jnp.maximum(n - 1, 0), 0))
    out = pl.pallas_call(
        _swa_kernel,
        grid=(batch, seq // w),
        in_specs=[pl.BlockSpec(memory_space=pltpu.SMEM),
                  pl.BlockSpec((None, w, nq), lambda b, n: (b, n, 0)),
                  prev, cur, prev, cur],
        out_specs=pl.BlockSpec((None, w, nq), lambda b, n: (b, n, 0)),
        out_shape=jax.ShapeDtypeStruct((batch, seq, nq), BF16),
        compiler_params=_params("arbitrary", "arbitrary"),
        name="swa_attn",
    )(sinks, q, k, k, v, v)
    return out.reshape(batch * seq, nq)


def kernel(x, positions, norm_pre_mix, norm_post_mix, norm_pre_mlp, norm_post_mlp, even_w_in, even_conv_w, even_lambda_q1, even_lambda_k1, even_lambda_q2, even_lambda_k2, even_subln_w, even_w_out, odd_w_qkv, odd_b_qkv, odd_sinks, odd_w_o, odd_b_o, mlp_w1, mlp_w2):
    batch, seq, d = x.shape
    n = batch * seq
    xf = x.reshape(n, d)
    cos_t, sin_t = _rope_tables(positions)
    row = lambda t: t.reshape(1, -1)

    lam_init = 0.8 - 0.6 * math.exp(-0.3 * 0)
    w1 = mlp_w1.astype(BF16)
    w2 = mlp_w2.astype(BF16)
    conv, q, k, v = _proj0(xf, row(norm_pre_mix[0]), even_w_in[0].astype(BF16), cos_t, sin_t,
                           even_conv_w[0], seq)
    lam_vecs = jnp.stack([even_lambda_q1[0], even_lambda_k1[0],
                          even_lambda_q2[0], even_lambda_k2[0]]).astype(F32)
    diff = _diff_attention(q, k, v, lam_vecs, row(even_subln_w[0]), lam_init, batch, seq)
    xf = _mix_mlp([conv, diff], even_w_out[0].astype(BF16), None, row(norm_post_mix[0]), xf,
                  row(norm_pre_mlp[0]), w1, w2, row(norm_post_mlp[0]), layer=0)

    q, k, v = _proj1(xf, row(norm_pre_mix[1]), odd_w_qkv[0].astype(BF16), row(odd_b_qkv[0]),
                     cos_t, sin_t)
    attn = _swa_attention(q, k, v, odd_sinks[0], batch, seq)
    xf = _mix_mlp([attn], odd_w_o[0].astype(BF16), row(odd_b_o[0]), row(norm_post_mix[1]), xf,
                  row(norm_pre_mlp[1]), w1, w2, row(norm_post_mlp[1]), layer=1)
    return xf.reshape(batch, seq, d)
```

```python
import functools
import math

import jax
import jax.numpy as jnp
from jax import lax
from jax.experimental import pallas as pl
from jax.experimental.pallas import tpu as pltpu

D_MODEL = 1024
HEAD_DIM = 64
ROT_HALF = 8
ROPE_THETA = 500000.0
RMS_EPS = 1e-6
CONV_WIDTH = 512
DIFF_HEADS = 4
DIFF_WIDTH = 512
DIFF_EPS = 1e-5
SWA_HEADS = 16
SWA_GROUP = 8
SWA_KV_HEADS = 2
WINDOW = 128
D_FF = 4096
LANES = 128
SUBLANES = 8
LOG2E = 1.4426950408889634
Q_SCALE = HEAD_DIM ** -0.5 * LOG2E

VMEM_LIMIT = 56 * 1024 * 1024

BF16 = jnp.bfloat16
F32 = jnp.float32
NT_DIMS = (((1,), (1,)), ((), ()))


def _params(*sem):
    return pltpu.CompilerParams(dimension_semantics=sem, vmem_limit_bytes=VMEM_LIMIT)


def _resident(shape):
    return pl.BlockSpec(shape, lambda *_: (0,) * len(shape), pipeline_mode=pl.Buffered(1))


def _rmsnorm(x, w, eps):
    return x * lax.rsqrt(jnp.mean(x * x, axis=-1, keepdims=True) + eps) * w


def _pair_layout(w):
    shape = w.shape
    w = w.reshape(shape[:-1] + (shape[-1] // LANES, LANES))
    w = jnp.concatenate([w[..., 0:8], w[..., 64:72], w[..., 16:64], w[..., 8:16], w[..., 72:]],
                        axis=-1)
    return w.reshape(shape)


def _first_head_lanes(shape):
    lane = lax.broadcasted_iota(jnp.int32, shape, 1)
    return (lane < ROT_HALF) | ((lane >= 2 * ROT_HALF) & (lane < HEAD_DIM + ROT_HALF))


def _rope(x, cos_t, sin_t):
    lane = lax.broadcasted_iota(jnp.int32, x.shape, 1)
    return x * cos_t + pltpu.roll(x, HEAD_DIM, 1) * jnp.where(lane < HEAD_DIM, -sin_t, sin_t)


def _rope_blocks(x, cos_t, sin_t, scale=None):
    out = []
    for j in range(x.shape[1] // LANES):
        r = _rope(x[:, j * LANES:(j + 1) * LANES], cos_t, sin_t)
        out.append((r if scale is None else r * scale).astype(BF16))
    return jnp.concatenate(out, axis=1)


def _rope_table_kernel(invf_ref, pos_ref, cos_ref, sin_ref):
    pos = pos_ref[...].astype(F32)
    for f in range(ROT_HALF):
        ang = pos * invf_ref[f]
        cos_ref[f] = jnp.cos(ang)
        sin_ref[f] = jnp.sin(ang)


def _rope_tables(positions):
    n = positions.size
    inv_freq = ROPE_THETA ** (-jnp.arange(0, 2 * ROT_HALF, 2, dtype=F32) / (2 * ROT_HALF))
    pos2d = positions.reshape(n // LANES, LANES)
    cos_c, sin_c = pl.pallas_call(
        _rope_table_kernel,
        out_shape=[jax.ShapeDtypeStruct((ROT_HALF, n // LANES, LANES), F32)] * 2,
        in_specs=[pl.BlockSpec(memory_space=pltpu.SMEM),
                  pl.BlockSpec(memory_space=pltpu.VMEM)],
        out_specs=[pl.BlockSpec(memory_space=pltpu.VMEM)] * 2,
        name="rope_tables",
    )(inv_freq, pos2d)

    rotary = (jnp.arange(LANES) % HEAD_DIM) < 2 * ROT_HALF

    def expand(t, fill):
        t = t.reshape(ROT_HALF, n).T
        return jnp.where(rotary, jnp.tile(t, (1, LANES // ROT_HALF)), fill)

    return expand(cos_c, 1.0), expand(sin_c, 0.0)


def _proj0_kernel(x_ref, nw_ref, w_ref, cos_ref, sin_ref, cw_ref,
                  conv_ref, q_ref, k_ref, v_ref, carry_ref, *, tiles_per_seq):
    i = pl.program_id(0)
    tm = x_ref.shape[0]
    c = CONV_WIDTH

    @pl.when(i == 0)
    def _():
        carry_ref[...] = jnp.zeros_like(carry_ref)

    hb = _rmsnorm(x_ref[...], nw_ref[...], RMS_EPS).astype(BF16)

    def proj(block):
        return jnp.dot(hb, w_ref[:, block * c:(block + 1) * c], preferred_element_type=F32)

    cos_t = cos_ref[...]
    sin_t = sin_ref[...]
    q_ref[...] = _rope_blocks(proj(3), cos_t, sin_t, Q_SCALE)
    u = proj(1) * proj(2)
    k_ref[...] = _rope_blocks(proj(4), cos_t, sin_t)
    gb = proj(0)

    prev = jnp.where(i % tiles_per_seq == 0, 0.0, carry_ref[...])
    row = lax.broadcasted_iota(jnp.int32, u.shape, 0)
    u1 = jnp.where(row == 0, prev[SUBLANES - 1:SUBLANES], pltpu.roll(u, 1, 0))
    u2 = jnp.where(row == 0, prev[SUBLANES - 2:SUBLANES - 1],
                   jnp.where(row == 1, prev[SUBLANES - 1:SUBLANES], pltpu.roll(u, 2, 0)))
    cw = cw_ref[...]
    conv = cw[0:1] * u2 + cw[1:2] * u1 + cw[2:3] * u
    conv_ref[...] = (gb * conv).astype(BF16)
    carry_ref[...] = u[tm - SUBLANES:, :]
    v_ref[...] = proj(5).astype(BF16)


def _proj0(x, nw, w, cos_t, sin_t, conv_w, seq, tm=512):
    n = x.shape[0]
    row = lambda width: pl.BlockSpec((tm, width), lambda i: (i, 0))
    out = jax.ShapeDtypeStruct((n, CONV_WIDTH), BF16)
    return pl.pallas_call(
        functools.partial(_proj0_kernel, tiles_per_seq=seq // tm),
        grid=(n // tm,),
        in_specs=[row(D_MODEL), _resident((1, D_MODEL)), _resident(w.shape),
                  row(LANES), row(LANES), _resident(conv_w.shape)],
        out_specs=[row(CONV_WIDTH)] * 4,
        out_shape=[out] * 4,
        scratch_shapes=[pltpu.VMEM((SUBLANES, CONV_WIDTH), F32)],
        compiler_params=_params("arbitrary"),
        name="proj0",
    )(x, nw, w, cos_t, sin_t, conv_w)


def _diff_attn_kernel(lam_ref, q_ref, k_ref, v_ref, sw_ref, o_ref,
                      qs_ref, va_ref, m_ref, acc_ref, *, lam_init, chain_rows):
    qi = pl.program_id(2)
    tq = q_ref.shape[0]
    hw = q_ref.shape[1]

    @pl.when(qi == 0)
    def _():
        va_ref[:, 0:hw] = v_ref[...]
        va_ref[:, hw:] = jnp.ones((va_ref.shape[0], hw), BF16)

    q = q_ref[...].astype(F32)
    map1 = _first_head_lanes(q.shape)
    qs_ref[0:tq, :] = jnp.where(map1, q, 0.0).astype(BF16)
    qs_ref[tq:, :] = jnp.where(map1, 0.0, q).astype(BF16)

    chains = [slice(r0, r0 + chain_rows) for r0 in range(0, 2 * tq, chain_rows)]
    cr = chain_rows
    tri = (lax.broadcasted_iota(jnp.int32, (cr, cr), 1)
           <= lax.broadcasted_iota(jnp.int32, (cr, cr), 0))

    def step(start, diagonal, first):
        widths = [(rows.start % tq) + cr if diagonal else tq for rows in chains]
        scores = [lax.dot_general(qs_ref[rows, :], k_ref[pl.ds(start, w), :], NT_DIMS,
                                  preferred_element_type=F32)
                  for rows, w in zip(chains, widths)]
        for rows, w, s in zip(chains, widths, scores):
            if diagonal:
                last = jnp.where(tri, s[:, w - cr:], -jnp.inf)
                s = last if w == cr else jnp.concatenate([s[:, :w - cr], last], axis=1)
            m_new = jnp.max(s, axis=-1, keepdims=True)
            if first:
                m_new = jnp.broadcast_to(m_new, (cr, LANES))
            else:
                m_prev = m_ref[rows, :]
                m_new = jnp.maximum(m_prev, m_new)
            p = jnp.exp2(s - jnp.concatenate([m_new] * (w // LANES), axis=1)).astype(BF16)
            pv = jnp.dot(p, va_ref[pl.ds(start, w), :], preferred_element_type=F32)
            if not first:
                alpha = jnp.exp2(m_prev - m_new)
                pv = jnp.concatenate([alpha, alpha], axis=1) * acc_ref[rows, :] + pv
            acc_ref[rows, :] = pv
            m_ref[rows, :] = m_new

    @pl.when(qi == 0)
    def _():
        step(0, True, True)

    @pl.when(qi > 0)
    def _():
        step(0, False, True)

        def body(kb, carry):
            step(pl.multiple_of(kb * tq, tq), False, False)
            return carry

        lax.fori_loop(1, qi, body, 0)
        step(pl.multiple_of(qi * tq, tq), True, False)

    lam_v = lam_ref[...]
    lam = (jnp.exp(jnp.sum(lam_v[0:1] * lam_v[1:2], keepdims=True))
           - jnp.exp(jnp.sum(lam_v[2:3] * lam_v[3:4], keepdims=True)) + lam_init)
    acc = acc_ref[...]
    o = acc[:, 0:hw] / acc[:, hw:]
    o = o[0:tq] - lam * o[tq:]
    o = _rmsnorm(o, sw_ref[...], DIFF_EPS) * (1.0 - lam_init)
    o_ref[...] = o.astype(o_ref.dtype)


def _diff_attention(q, k, v, lam_vecs, subln_w, lam_init, batch, seq, tq=1024,
                    chain_rows=256):
    q, k, v = (t.reshape(batch, seq, DIFF_WIDTH) for t in (q, k, v))
    hw = 2 * HEAD_DIM
    out = pl.pallas_call(
        functools.partial(_diff_attn_kernel, lam_init=lam_init, chain_rows=chain_rows),
        grid=(batch, DIFF_HEADS, seq // tq),
        in_specs=[_resident(lam_vecs.shape),
                  pl.BlockSpec((None, tq, hw), lambda b, h, i: (b, i, h)),
                  pl.BlockSpec((None, seq, hw), lambda b, h, i: (b, 0, h)),
                  pl.BlockSpec((None, seq, hw), lambda b, h, i: (b, 0, h)),
                  _resident(subln_w.shape)],
        out_specs=pl.BlockSpec((None, tq, hw), lambda b, h, i: (b, i, h)),
        out_shape=jax.ShapeDtypeStruct((batch, seq, DIFF_WIDTH), BF16),
        scratch_shapes=[pltpu.VMEM((2 * tq, hw), BF16),
                        pltpu.VMEM((seq, 2 * hw), BF16),
                        pltpu.VMEM((2 * tq, hw), F32),
                        pltpu.VMEM((2 * tq, 2 * hw), F32)],
        compiler_params=_params("arbitrary", "arbitrary", "arbitrary"),
        name="diff_attn",
    )(lam_vecs, q, k, v, subln_w)
    return out.reshape(batch * seq, DIFF_WIDTH)


def _mix_mlp_kernel(*refs, n_in, has_bias, ff_chunk):
    a_refs = refs[:n_in]
    wo_ref = refs[n_in]
    b_ref = refs[n_in + 1] if has_bias else None
    nwo_ref, x_ref, nw1_ref, w1_ref, w2_ref, nw2_ref, o_ref = refs[n_in + 1 + has_bias:]
    a = jnp.concatenate([r[...] for r in a_refs], axis=1) if n_in > 1 else a_refs[0][...]
    h = jnp.dot(a, wo_ref[...], preferred_element_type=F32)
    if has_bias:
        h = h + b_ref[...]
    x1 = x_ref[...] + _rmsnorm(h, nwo_ref[...], RMS_EPS)

    hb = _rmsnorm(x1, nw1_ref[...], RMS_EPS).astype(BF16)
    acc = None
    for c0 in range(0, D_FF, ff_chunk):
        t = jnp.dot(hb, w1_ref[:, c0:c0 + ff_chunk], preferred_element_type=F32)
        t = jnp.square(jnp.maximum(t, 0.0)).astype(BF16)
        part = jnp.dot(t, w2_ref[c0:c0 + ff_chunk, :], preferred_element_type=F32)
        acc = part if acc is None else acc + part
    o_ref[...] = x1 + _rmsnorm(acc, nw2_ref[...], RMS_EPS)


def _mix_mlp(acts, wo, bias, nwo, x, nw1, w1, w2, nw2, layer, tm=512, ff_chunk=1024):
    n = x.shape[0]
    row = lambda width: pl.BlockSpec((tm, width), lambda i: (i, 0))
    stacked = lambda w: pl.BlockSpec((None,) + w.shape[1:], lambda i: (layer, 0, 0),
                                     pipeline_mode=pl.Buffered(1))
    has_bias = bias is not None
    operands = (list(acts) + [wo] + ([bias] if has_bias else [])
                + [nwo, x, nw1, w1, w2, nw2])
    in_specs = ([row(a.shape[1]) for a in acts] + [_resident(wo.shape)]
                + ([_resident(bias.shape)] if has_bias else [])
                + [_resident(nwo.shape), row(D_MODEL), _resident(nw1.shape),
                   stacked(w1), stacked(w2), _resident(nw2.shape)])
    return pl.pallas_call(
        functools.partial(_mix_mlp_kernel, n_in=len(acts), has_bias=has_bias, ff_chunk=ff_chunk),
        grid=(n // tm,),
        in_specs=in_specs,
        out_specs=row(D_MODEL),
        out_shape=jax.ShapeDtypeStruct((n, D_MODEL), F32),
        compiler_params=_params("arbitrary"),
        name="mix_mlp",
    )(*operands)


def _proj1_kernel(x_ref, nw_ref, w_ref, b_ref, cos_ref, sin_ref, q_ref, k_ref, v_ref):
    hb = _rmsnorm(x_ref[...], nw_ref[...], RMS_EPS).astype(BF16)

    def proj(c0, width):
        return (jnp.dot(hb, w_ref[:, c0:c0 + width], preferred_element_type=F32)
                + b_ref[:, c0:c0 + width])

    cos_t = cos_ref[...]
    sin_t = sin_ref[...]
    nq = SWA_HEADS * HEAD_DIM
    kv = proj(nq, 2 * LANES)
    chunk = 2 * LANES
    for c0 in range(0, nq, chunk):
        q_ref[:, c0:c0 + chunk] = _rope_blocks(proj(c0, chunk), cos_t, sin_t, Q_SCALE)

    k = _rope(kv[:, 0:LANES], cos_t, sin_t)
    lane = lax.broadcasted_iota(jnp.int32, k.shape, 1)
    head_a = _first_head_lanes(k.shape)
    far = pltpu.roll(k, HEAD_DIM, 1)
    k0 = jnp.where(head_a, k, jnp.where(lane >= HEAD_DIM + 2 * ROT_HALF, far,
                                        pltpu.roll(k, ROT_HALF, 1)))
    k1 = jnp.where(head_a, jnp.where((lane >= 2 * ROT_HALF) & (lane < HEAD_DIM), far,
                                     pltpu.roll(k, LANES - ROT_HALF, 1)), k)
    k_ref[...] = jnp.concatenate([k0, k1], axis=1).astype(BF16)

    v = kv[:, LANES:]
    swapped = pltpu.roll(v, HEAD_DIM, 1)
    low = lane < HEAD_DIM
    v_ref[...] = jnp.concatenate([jnp.where(low, v, swapped), jnp.where(low, swapped, v)],
                                 axis=1).astype(BF16)


def _proj1(x, nw, w, b, cos_t, sin_t, tm=512):
    n = x.shape[0]
    row = lambda width: pl.BlockSpec((tm, width), lambda i: (i, 0))
    nq = SWA_HEADS * HEAD_DIM
    return pl.pallas_call(
        _proj1_kernel,
        grid=(n // tm,),
        in_specs=[row(D_MODEL), _resident(nw.shape), _resident(w.shape), _resident(b.shape),
                  row(LANES), row(LANES)],
        out_specs=[row(nq), row(2 * LANES), row(2 * LANES)],
        out_shape=[jax.ShapeDtypeStruct((n, nq), BF16),
                   jax.ShapeDtypeStruct((n, 2 * LANES), BF16),
                   jax.ShapeDtypeStruct((n, 2 * LANES), BF16)],
        compiler_params=_params("arbitrary"),
        name="proj1",
    )(x, nw, w, b, cos_t, sin_t)


def _swa_kernel(sink_ref, q_ref, kp_ref, kc_ref, vp_ref, vc_ref, o_ref):
    n = pl.program_id(1)
    w = WINDOW
    i = lax.broadcasted_iota(jnp.int32, (w, 2 * w), 0)
    j = lax.broadcasted_iota(jnp.int32, (w, 2 * w), 1)
    dist = i + w - j
    valid = (dist >= 0) & (dist < w) & ((n * w - w + j) >= 0)
    bias = jnp.where(valid, 0.0, -jnp.inf)
    low = lax.broadcasted_iota(jnp.int32, (w, LANES), 1) < HEAD_DIM
    head_a = _first_head_lanes((w, LANES))
    ones = jnp.ones((2 * w, LANES), BF16)
    pairs = SWA_GROUP // 2

    for kh in range(SWA_KV_HEADS):
        cols = slice(kh * LANES, (kh + 1) * LANES)
        k = jnp.concatenate([kp_ref[:, cols], kc_ref[:, cols]], axis=0)
        va = jnp.concatenate(
            [jnp.concatenate([vp_ref[:, cols], vc_ref[:, cols]], axis=0), ones], axis=1)
        rows = []
        for p in range(pairs):
            c0 = (kh * pairs + p) * LANES
            q2 = q_ref[:, c0:c0 + LANES].astype(F32)
            rows.append(jnp.where(head_a, q2, 0.0).astype(BF16))
            rows.append(jnp.where(head_a, 0.0, q2).astype(BF16))
        qs = jnp.concatenate(rows, axis=0)
        s = lax.dot_general(qs, k, NT_DIMS, preferred_element_type=F32)

        probs = []
        sink_terms = []
        for g in range(SWA_GROUP):
            sink = sink_ref[kh * SWA_GROUP + g] * LOG2E
            sg = s[g * w:(g + 1) * w] + bias
            m = jnp.maximum(jnp.max(sg, axis=-1, keepdims=True), sink)
            m = jnp.broadcast_to(m, (w, LANES))
            probs.append(jnp.exp2(sg - jnp.concatenate([m, m], axis=1)).astype(BF16))
            sink_terms.append(jnp.exp2(sink - m))
        pv = jnp.dot(jnp.concatenate(probs, axis=0), va, preferred_element_type=F32)

        outs = []
        for g in range(SWA_GROUP):
            blk = pv[g * w:(g + 1) * w]
            outs.append(blk[:, 0:LANES] / (blk[:, LANES:] + sink_terms[g]))
        for p in range(pairs):
            c0 = (kh * pairs + p) * LANES
            o_ref[:, c0:c0 + LANES] = jnp.where(low, outs[2 * p], outs[2 * p + 1]).astype(o_ref.dtype)


def _swa_attention(q, k, v, sinks, batch, seq):
    nq = SWA_HEADS * HEAD_DIM
    q = q.reshape(batch, seq, nq)
    k = k.reshape(batch, seq, 2 * LANES)
    v = v.reshape(batch, seq, 2 * LANES)
    w = WINDOW
    cur = pl.BlockSpec((None, w, 2 * LANES), lambda b, n: (b, n, 0))
    prev = pl.BlockSpec((None, w, 2 * LANES), lambda b, n: (b, jnp.maximum(n - 1, 0), 0))
    out = pl.pallas_call(
        _swa_kernel,
        grid=(batch, seq // w),
        in_specs=[pl.BlockSpec(memory_space=pltpu.SMEM),
                  pl.BlockSpec((None, w, nq), lambda b, n: (b, n, 0)),
                  prev, cur, prev, cur],
        out_specs=pl.BlockSpec((None, w, nq), lambda b, n: (b, n, 0)),
        out_shape=jax.ShapeDtypeStruct((batch, seq, nq), BF16),
        compiler_params=_params("arbitrary", "arbitrary"),
        name="swa_attn",
    )(sinks, q, k, k, v, v)
    return out.reshape(batch * seq, nq)


def kernel(x, positions, norm_pre_mix, norm_post_mix, norm_pre_mlp, norm_post_mlp, even_w_in, even_conv_w, even_lambda_q1, even_lambda_k1, even_lambda_q2, even_lambda_k2, even_subln_w, even_w_out, odd_w_qkv, odd_b_qkv, odd_sinks, odd_w_o, odd_b_o, mlp_w1, mlp_w2):
    batch, seq, d = x.shape
    n = batch * seq
    xf = x.reshape(n, d)
    cos_t, sin_t = _rope_tables(positions)
    row = lambda t: t.reshape(1, -1)

    lam_init = 0.8 - 0.6 * math.exp(-0.3 * 0)
    w1 = mlp_w1.astype(BF16)
    w2 = mlp_w2.astype(BF16)
    qk0 = slice(3 * CONV_WIDTH, 3 * CONV_WIDTH + 2 * DIFF_WIDTH)
    w_in = even_w_in[0]
    w_in = jnp.concatenate([w_in[:, :qk0.start], _pair_layout(w_in[:, qk0]), w_in[:, qk0.stop:]],
                           axis=1).astype(BF16)
    conv, q, k, v = _proj0(xf, row(norm_pre_mix[0]), w_in, cos_t, sin_t, even_conv_w[0], seq)
    lam_vecs = jnp.stack([even_lambda_q1[0], even_lambda_k1[0],
                          even_lambda_q2[0], even_lambda_k2[0]]).astype(F32)
    diff = _diff_attention(q, k, v, lam_vecs, row(even_subln_w[0]), lam_init, batch, seq)
    xf = _mix_mlp([conv, diff], even_w_out[0].astype(BF16), None, row(norm_post_mix[0]), xf,
                  row(norm_pre_mlp[0]), w1, w2, row(norm_post_mlp[0]), layer=0)

    nqk = (SWA_HEADS + SWA_KV_HEADS) * HEAD_DIM
    qkv_layout = lambda t: jnp.concatenate([_pair_layout(t[..., :nqk]), t[..., nqk:]], axis=-1)
    q, k, v = _proj1(xf, row(norm_pre_mix[1]), qkv_layout(odd_w_qkv[0]).astype(BF16),
                     qkv_layout(row(odd_b_qkv[0])), cos_t, sin_t)
    attn = _swa_attention(q, k, v, odd_sinks[0], batch, seq)
    xf = _mix_mlp([attn], odd_w_o[0].astype(BF16), row(odd_b_o[0]), row(norm_post_mix[1]), xf,
                  row(norm_pre_mlp[1]), w1, w2, row(norm_post_mlp[1]), layer=1)
    return xf.reshape(batch, seq, d)
```

```python
import functools
import math

import jax
import jax.numpy as jnp
from jax import lax
from jax.experimental import pallas as pl
from jax.experimental.pallas import tpu as pltpu

D_MODEL = 1024
HEAD_DIM = 64
ROT_HALF = 8
ROPE_THETA = 500000.0
RMS_EPS = 1e-6
CONV_WIDTH = 512
DIFF_HEADS = 4
DIFF_WIDTH = 512
DIFF_EPS = 1e-5
SWA_HEADS = 16
SWA_GROUP = 8
SWA_KV_HEADS = 2
WINDOW = 128
D_FF = 4096
LANES = 128
SUBLANES = 8
LOG2E = 1.4426950408889634
Q_SCALE = HEAD_DIM ** -0.5 * LOG2E

VMEM_LIMIT = 56 * 1024 * 1024

BF16 = jnp.bfloat16
F32 = jnp.float32
NT_DIMS = (((1,), (1,)), ((), ()))


def _params(*sem):
    return pltpu.CompilerParams(dimension_semantics=sem, vmem_limit_bytes=VMEM_LIMIT)


def _resident(shape):
    return pl.BlockSpec(shape, lambda *_: (0,) * len(shape), pipeline_mode=pl.Buffered(1))


def _rmsnorm(x, w, eps):
    return x * lax.rsqrt(jnp.mean(x * x, axis=-1, keepdims=True) + eps) * w


def _pair_layout(w):
    shape = w.shape
    w = w.reshape(shape[:-1] + (shape[-1] // LANES, LANES))
    w = jnp.concatenate([w[..., 0:8], w[..., 64:72], w[..., 16:64], w[..., 8:16], w[..., 72:]],
                        axis=-1)
    return w.reshape(shape)


def _first_head_lanes(shape):
    lane = lax.broadcasted_iota(jnp.int32, shape, 1)
    return (lane < ROT_HALF) | ((lane >= 2 * ROT_HALF) & (lane < HEAD_DIM + ROT_HALF))


def _rope(x, cos_t, sin_t):
    lane = lax.broadcasted_iota(jnp.int32, x.shape, 1)
    return x * cos_t + pltpu.roll(x, HEAD_DIM, 1) * jnp.where(lane < HEAD_DIM, -sin_t, sin_t)


def _rope_blocks(x, cos_t, sin_t, scale=None):
    out = []
    for j in range(x.shape[1] // LANES):
        r = _rope(x[:, j * LANES:(j + 1) * LANES], cos_t, sin_t)
        out.append((r if scale is None else r * scale).astype(BF16))
    return jnp.concatenate(out, axis=1)


def _rope_table_kernel(invf_ref, pos_ref, cos_ref, sin_ref):
    pos = pos_ref[...].astype(F32)
    for f in range(ROT_HALF):
        ang = pos * invf_ref[f]
        cos_ref[f] = jnp.cos(ang)
        sin_ref[f] = jnp.sin(ang)


def _rope_tables(positions):
    n = positions.size
    inv_freq = ROPE_THETA ** (-jnp.arange(0, 2 * ROT_HALF, 2, dtype=F32) / (2 * ROT_HALF))
    pos2d = positions.reshape(n // LANES, LANES)
    cos_c, sin_c = pl.pallas_call(
        _rope_table_kernel,
        out_shape=[jax.ShapeDtypeStruct((ROT_HALF, n // LANES, LANES), F32)] * 2,
        in_specs=[pl.BlockSpec(memory_space=pltpu.SMEM),
                  pl.BlockSpec(memory_space=pltpu.VMEM)],
        out_specs=[pl.BlockSpec(memory_space=pltpu.VMEM)] * 2,
        name="rope_tables",
    )(inv_freq, pos2d)

    rotary = (jnp.arange(LANES) % HEAD_DIM) < 2 * ROT_HALF

    def expand(t, fill):
        t = t.reshape(ROT_HALF, n).T
        return jnp.where(rotary, jnp.tile(t, (1, LANES // ROT_HALF)), fill)

    return expand(cos_c, 1.0), expand(sin_c, 0.0)


def _proj0_kernel(x_ref, nw_ref, w_ref, cos_ref, sin_ref, cw_ref,
                  conv_ref, q_ref, k_ref, v_ref, carry_ref, *, tiles_per_seq):
    i = pl.program_id(0)
    tm = x_ref.shape[0]
    c = CONV_WIDTH

    @pl.when(i == 0)
    def _():
        carry_ref[...] = jnp.zeros_like(carry_ref)

    hb = _rmsnorm(x_ref[...], nw_ref[...], RMS_EPS).astype(BF16)

    def proj(block):
        return jnp.dot(hb, w_ref[:, block * c:(block + 1) * c], preferred_element_type=F32)

    cos_t = cos_ref[...]
    sin_t = sin_ref[...]
    q_ref[...] = _rope_blocks(proj(3), cos_t, sin_t, Q_SCALE)
    u = proj(1) * proj(2)
    k_ref[...] = _rope_blocks(proj(4), cos_t, sin_t)
    gb = proj(0)

    prev = jnp.where(i % tiles_per_seq == 0, 0.0, carry_ref[...])
    row = lax.broadcasted_iota(jnp.int32, u.shape, 0)
    u1 = jnp.where(row == 0, prev[SUBLANES - 1:SUBLANES], pltpu.roll(u, 1, 0))
    u2 = jnp.where(row == 0, prev[SUBLANES - 2:SUBLANES - 1],
                   jnp.where(row == 1, prev[SUBLANES - 1:SUBLANES], pltpu.roll(u, 2, 0)))
    cw = cw_ref[...]
    conv = cw[0:1] * u2 + cw[1:2] * u1 + cw[2:3] * u
    conv_ref[...] = (gb * conv).astype(BF16)
    carry_ref[...] = u[tm - SUBLANES:, :]
    v_ref[...] = proj(5).astype(BF16)


def _proj0(x, nw, w, cos_t, sin_t, conv_w, seq, tm=512):
    n = x.shape[0]
    row = lambda width: pl.BlockSpec((tm, width), lambda i: (i, 0))
    out = jax.ShapeDtypeStruct((n, CONV_WIDTH), BF16)
    return pl.pallas_call(
        functools.partial(_proj0_kernel, tiles_per_seq=seq // tm),
        grid=(n // tm,),
        in_specs=[row(D_MODEL), _resident((1, D_MODEL)), _resident(w.shape),
                  row(LANES), row(LANES), _resident(conv_w.shape)],
        out_specs=[row(CONV_WIDTH)] * 4,
        out_shape=[out] * 4,
        scratch_shapes=[pltpu.VMEM((SUBLANES, CONV_WIDTH), F32)],
        compiler_params=_params("arbitrary"),
        name="proj0",
    )(x, nw, w, cos_t, sin_t, conv_w)


def _diff_attn_kernel(lam_ref, q_ref, k_ref, v_ref, sw_ref, o_ref,
                      qs_ref, va_ref, m_ref, acc_ref, *, lam_init, chain_rows):
    qi = pl.program_id(2)
    tq = q_ref.shape[0]
    hw = q_ref.shape[1]

    @pl.when(qi == 0)
    def _():
        va_ref[:, 0:hw] = v_ref[...]
        va_ref[:, hw:] = jnp.ones((va_ref.shape[0], hw), BF16)

    q = q_ref[...]
    map1 = jnp.where(_first_head_lanes((1, hw)), 1.0, 0.0).astype(BF16)
    qs_ref[0:tq, :] = q * map1
    qs_ref[tq:, :] = q * (1.0 - map1)

    chains = [slice(r0, r0 + chain_rows) for r0 in range(0, 2 * tq, chain_rows)]
    cr = chain_rows
    tri = (lax.broadcasted_iota(jnp.int32, (cr, cr), 1)
           <= lax.broadcasted_iota(jnp.int32, (cr, cr), 0))

    def step(start, diagonal, first):
        widths = [(rows.start % tq) + cr if diagonal else tq for rows in chains]
        scores = [lax.dot_general(qs_ref[rows, :], k_ref[pl.ds(start, w), :], NT_DIMS,
                                  preferred_element_type=F32)
                  for rows, w in zip(chains, widths)]
        for rows, w, s in zip(chains, widths, scores):
            if diagonal:
                last = jnp.where(tri, s[:, w - cr:], -jnp.inf)
                s = last if w == cr else jnp.concatenate([s[:, :w - cr], last], axis=1)
            m_new = jnp.max(s, axis=-1, keepdims=True)
            if first:
                m_new = jnp.broadcast_to(m_new, (cr, LANES))
            else:
                m_prev = m_ref[rows, :]
                m_new = jnp.maximum(m_prev, m_new)
            p = jnp.exp2(s - jnp.concatenate([m_new] * (w // LANES), axis=1)).astype(BF16)
            pv = jnp.dot(p, va_ref[pl.ds(start, w), :], preferred_element_type=F32)
            if not first:
                alpha = jnp.exp2(m_prev - m_new)
                pv = jnp.concatenate([alpha, alpha], axis=1) * acc_ref[rows, :] + pv
            acc_ref[rows, :] = pv
            m_ref[rows, :] = m_new

    @pl.when(qi == 0)
    def _():
        step(0, True, True)

    @pl.when(qi > 0)
    def _():
        step(0, False, True)

        def body(kb, carry):
            step(pl.multiple_of(kb * tq, tq), False, False)
            return carry

        lax.fori_loop(1, qi, body, 0)
        step(pl.multiple_of(qi * tq, tq), True, False)

    lam_v = lam_ref[...]
    lam = (jnp.exp(jnp.sum(lam_v[0:1] * lam_v[1:2], keepdims=True))
           - jnp.exp(jnp.sum(lam_v[2:3] * lam_v[3:4], keepdims=True)) + lam_init)
    acc = acc_ref[...]
    o = acc[:, 0:hw] / acc[:, hw:]
    o = o[0:tq] - lam * o[tq:]
    o = _rmsnorm(o, sw_ref[...], DIFF_EPS) * (1.0 - lam_init)
    o_ref[...] = o.astype(o_ref.dtype)


def _diff_attention(q, k, v, lam_vecs, subln_w, lam_init, batch, seq, tq=1024,
                    chain_rows=256):
    q, k, v = (t.reshape(batch, seq, DIFF_WIDTH) for t in (q, k, v))
    hw = 2 * HEAD_DIM
    out = pl.pallas_call(
        functools.partial(_diff_attn_kernel, lam_init=lam_init, chain_rows=chain_rows),
        grid=(batch, DIFF_HEADS, seq // tq),
        in_specs=[_resident(lam_vecs.shape),
                  pl.BlockSpec((None, tq, hw), lambda b, h, i: (b, i, h)),
                  pl.BlockSpec((None, seq, hw), lambda b, h, i: (b, 0, h)),
                  pl.BlockSpec((None, seq, hw), lambda b, h, i: (b, 0, h)),
                  _resident(subln_w.shape)],
        out_specs=pl.BlockSpec((None, tq, hw), lambda b, h, i: (b, i, h)),
        out_shape=jax.ShapeDtypeStruct((batch, seq, DIFF_WIDTH), BF16),
        scratch_shapes=[pltpu.VMEM((2 * tq, hw), BF16),
                        pltpu.VMEM((seq, 2 * hw), BF16),
                        pltpu.VMEM((2 * tq, hw), F32),
                        pltpu.VMEM((2 * tq, 2 * hw), F32)],
        compiler_params=_params("arbitrary", "arbitrary", "arbitrary"),
        name="diff_attn",
    )(lam_vecs, q, k, v, subln_w)
    return out.reshape(batch * seq, DIFF_WIDTH)


def _mix_mlp_kernel(*refs, n_in, has_bias, ff_chunk):
    a_refs = refs[:n_in]
    wo_ref = refs[n_in]
    b_ref = refs[n_in + 1] if has_bias else None
    nwo_ref, x_ref, nw1_ref, w1_ref, w2_ref, nw2_ref, o_ref = refs[n_in + 1 + has_bias:]
    a = jnp.concatenate([r[...] for r in a_refs], axis=1) if n_in > 1 else a_refs[0][...]
    h = jnp.dot(a, wo_ref[...], preferred_element_type=F32)
    if has_bias:
        h = h + b_ref[...]
    x1 = x_ref[...] + _rmsnorm(h, nwo_ref[...], RMS_EPS)

    hb = _rmsnorm(x1, nw1_ref[...], RMS_EPS).astype(BF16)
    acc = None
    for c0 in range(0, D_FF, ff_chunk):
        t = jnp.dot(hb, w1_ref[:, c0:c0 + ff_chunk], preferred_element_type=F32)
        t = jnp.square(jnp.maximum(t, 0.0)).astype(BF16)
        part = jnp.dot(t, w2_ref[c0:c0 + ff_chunk, :], preferred_element_type=F32)
        acc = part if acc is None else acc + part
    o_ref[...] = x1 + _rmsnorm(acc, nw2_ref[...], RMS_EPS)


def _mix_mlp(acts, wo, bias, nwo, x, nw1, w1, w2, nw2, layer, tm=512, ff_chunk=1024):
    n = x.shape[0]
    row = lambda width: pl.BlockSpec((tm, width), lambda i: (i, 0))
    stacked = lambda w: pl.BlockSpec((None,) + w.shape[1:], lambda i: (layer, 0, 0),
                                     pipeline_mode=pl.Buffered(1))
    has_bias = bias is not None
    operands = (list(acts) + [wo] + ([bias] if has_bias else [])
                + [nwo, x, nw1, w1, w2, nw2])
    in_specs = ([row(a.shape[1]) for a in acts] + [_resident(wo.shape)]
                + ([_resident(bias.shape)] if has_bias else [])
                + [_resident(nwo.shape), row(D_MODEL), _resident(nw1.shape),
                   stacked(w1), stacked(w2), _resident(nw2.shape)])
    return pl.pallas_call(
        functools.partial(_mix_mlp_kernel, n_in=len(acts), has_bias=has_bias, ff_chunk=ff_chunk),
        grid=(n // tm,),
        in_specs=in_specs,
        out_specs=row(D_MODEL),
        out_shape=jax.ShapeDtypeStruct((n, D_MODEL), F32),
        compiler_params=_params("arbitrary"),
        name="mix_mlp",
    )(*operands)


def _proj1_kernel(x_ref, nw_ref, w_ref, b_ref, cos_ref, sin_ref, q_ref, k_ref, v_ref):
    hb = _rmsnorm(x_ref[...], nw_ref[...], RMS_EPS).astype(BF16)

    def proj(c0, width):
        return (jnp.dot(hb, w_ref[:, c0:c0 + width], preferred_element_type=F32)
                + b_ref[:, c0:c0 + width])

    cos_t = cos_ref[...]
    sin_t = sin_ref[...]
    nq = SWA_HEADS * HEAD_DIM
    kv = proj(nq, 2 * LANES)
    chunk = 2 * LANES
    for c0 in range(0, nq, chunk):
        q_ref[:, c0:c0 + chunk] = _rope_blocks(proj(c0, chunk), cos_t, sin_t, Q_SCALE)

    k = _rope(kv[:, 0:LANES], cos_t, sin_t)
    lane = lax.broadcasted_iota(jnp.int32, k.shape, 1)
    head_a = _first_head_lanes(k.shape)
    far = pltpu.roll(k, HEAD_DIM, 1)
    k0 = jnp.where(head_a, k, jnp.where(lane >= HEAD_DIM + 2 * ROT_HALF, far,
                                        pltpu.roll(k, ROT_HALF, 1)))
    k1 = jnp.where(head_a, jnp.where((lane >= 2 * ROT_HALF) & (lane < HEAD_DIM), far,
                                     pltpu.roll(k, LANES - ROT_HALF, 1)), k)
    k_ref[...] = jnp.concatenate([k0, k1], axis=1).astype(BF16)

    v = kv[:, LANES:]
    swapped = pltpu.roll(v, HEAD_DIM, 1)
    low = lane < HEAD_DIM
    v_ref[...] = jnp.concatenate([jnp.where(low, v, swapped), jnp.where(low, swapped, v)],
                                 axis=1).astype(BF16)


def _proj1(x, nw, w, b, cos_t, sin_t, tm=512):
    n = x.shape[0]
    row = lambda width: pl.BlockSpec((tm, width), lambda i: (i, 0))
    nq = SWA_HEADS * HEAD_DIM
    return pl.pallas_call(
        _proj1_kernel,
        grid=(n // tm,),
        in_specs=[row(D_MODEL), _resident(nw.shape), _resident(w.shape), _resident(b.shape),
                  row(LANES), row(LANES)],
        out_specs=[row(nq), row(2 * LANES), row(2 * LANES)],
        out_shape=[jax.ShapeDtypeStruct((n, nq), BF16),
                   jax.ShapeDtypeStruct((n, 2 * LANES), BF16),
                   jax.ShapeDtypeStruct((n, 2 * LANES), BF16)],
        compiler_params=_params("arbitrary"),
        name="proj1",
    )(x, nw, w, b, cos_t, sin_t)


def _swa_kernel(sink_ref, q_ref, kp_ref, kc_ref, vp_ref, vc_ref, o_ref, *, blocks):
    n = pl.program_id(1)
    w = WINDOW
    i = lax.broadcasted_iota(jnp.int32, (w, 2 * w), 0)
    j = lax.broadcasted_iota(jnp.int32, (w, 2 * w), 1)
    dist = i + w - j
    band = (dist >= 0) & (dist < w)
    bias_first = jnp.where(band & ((n > 0) | (j >= w)), 0.0, -jnp.inf)
    bias_rest = jnp.where(band, 0.0, -jnp.inf)
    low = lax.broadcasted_iota(jnp.int32, (w, LANES), 1) < HEAD_DIM
    head_a = _first_head_lanes((w, LANES))
    ones = jnp.ones((2 * w, LANES), BF16)
    pairs = SWA_GROUP // 2

    def window(prev_ref, cur_ref, t, cols):
        if t == 0:
            return jnp.concatenate([prev_ref[:, cols], cur_ref[0:w, cols]], axis=0)
        return cur_ref[(t - 1) * w:(t + 1) * w, cols]

    chains = [(t, kh) for t in range(blocks) for kh in range(SWA_KV_HEADS)]
    scores = []
    for t, kh in chains:
        cols = slice(kh * LANES, (kh + 1) * LANES)
        rows = []
        for p in range(pairs):
            c0 = (kh * pairs + p) * LANES
            q2 = q_ref[t * w:(t + 1) * w, c0:c0 + LANES].astype(F32)
            rows.append(jnp.where(head_a, q2, 0.0).astype(BF16))
            rows.append(jnp.where(head_a, 0.0, q2).astype(BF16))
        qs = jnp.concatenate(rows, axis=0)
        scores.append(lax.dot_general(qs, window(kp_ref, kc_ref, t, cols), NT_DIMS,
                                      preferred_element_type=F32))

    for (t, kh), s in zip(chains, scores):
        cols = slice(kh * LANES, (kh + 1) * LANES)
        bias = bias_first if t == 0 else bias_rest
        va = jnp.concatenate([window(vp_ref, vc_ref, t, cols), ones], axis=1)
        probs = []
        sink_terms = []
        for g in range(SWA_GROUP):
            sink = sink_ref[kh * SWA_GROUP + g] * LOG2E
            sg = s[g * w:(g + 1) * w] + bias
            m = jnp.maximum(jnp.max(sg, axis=-1, keepdims=True), sink)
            m = jnp.broadcast_to(m, (w, LANES))
            probs.append(jnp.exp2(sg - jnp.concatenate([m, m], axis=1)).astype(BF16))
            sink_terms.append(jnp.exp2(sink - m))
        pv = jnp.dot(jnp.concatenate(probs, axis=0), va, preferred_element_type=F32)

        outs = []
        for g in range(SWA_GROUP):
            blk = pv[g * w:(g + 1) * w]
            outs.append(blk[:, 0:LANES] / (blk[:, LANES:] + sink_terms[g]))
        for p in range(pairs):
            c0 = (kh * pairs + p) * LANES
            o_ref[t * w:(t + 1) * w, c0:c0 + LANES] = jnp.where(
                low, outs[2 * p], outs[2 * p + 1]).astype(o_ref.dtype)


def _swa_attention(q, k, v, sinks, batch, seq, blocks=8):
    nq = SWA_HEADS * HEAD_DIM
    q = q.reshape(batch, seq, nq)
    k = k.reshape(batch, seq, 2 * LANES)
    v = v.reshape(batch, seq, 2 * LANES)
    w = WINDOW
    rows = blocks * w
    cur = pl.BlockSpec((None, rows, 2 * LANES), lambda b, n: (b, n, 0))
    prev = pl.BlockSpec((None, w, 2 * LANES), lambda b, n: (b, jnp.maximum(blocks * n - 1, 0), 0))
    out = pl.pallas_call(
        functools.partial(_swa_kernel, blocks=blocks),
        grid=(batch, seq // rows),
        in_specs=[pl.BlockSpec(memory_space=pltpu.SMEM),
                  pl.BlockSpec((None, rows, nq), lambda b, n: (b, n, 0)),
                  prev, cur, prev, cur],
        out_specs=pl.BlockSpec((None, rows, nq), lambda b, n: (b, n, 0)),
        out_shape=jax.ShapeDtypeStruct((batch, seq, nq), BF16),
        compiler_params=_params("arbitrary", "arbitrary"),
        name="swa_attn",
    )(sinks, q, k, k, v, v)
    return out.reshape(batch * seq, nq)


def kernel(x, positions, norm_pre_mix, norm_post_mix, norm_pre_mlp, norm_post_mlp, even_w_in, even_conv_w, even_lambda_q1, even_lambda_k1, even_lambda_q2, even_lambda_k2, even_subln_w, even_w_out, odd_w_qkv, odd_b_qkv, odd_sinks, odd_w_o, odd_b_o, mlp_w1, mlp_w2):
    batch, seq, d = x.shape
    n = batch * seq
    xf = x.reshape(n, d)
    cos_t, sin_t = _rope_tables(positions)
    row = lambda t: t.reshape(1, -1)

    lam_init = 0.8 - 0.6 * math.exp(-0.3 * 0)
    w1 = mlp_w1.astype(BF16)
    w2 = mlp_w2.astype(BF16)
    qk0 = slice(3 * CONV_WIDTH, 3 * CONV_WIDTH + 2 * DIFF_WIDTH)
    w_in = even_w_in[0].astype(BF16)
    w_in = jnp.concatenate([w_in[:, :qk0.start], _pair_layout(w_in[:, qk0]), w_in[:, qk0.stop:]],
                           axis=1)
    conv, q, k, v = _proj0(xf, row(norm_pre_mix[0]), w_in, cos_t, sin_t, even_conv_w[0], seq)
    lam_vecs = jnp.stack([even_lambda_q1[0], even_lambda_k1[0],
                          even_lambda_q2[0], even_lambda_k2[0]]).astype(F32)
    diff = _diff_attention(q, k, v, lam_vecs, row(even_subln_w[0]), lam_init, batch, seq)
    xf = _mix_mlp([conv, diff], even_w_out[0].astype(BF16), None, row(norm_post_mix[0]), xf,
                  row(norm_pre_mlp[0]), w1, w2, row(norm_post_mlp[0]), layer=0)

    nqk = (SWA_HEADS + SWA_KV_HEADS) * HEAD_DIM
    qkv_layout = lambda t: jnp.concatenate([_pair_layout(t[..., :nqk]), t[..., nqk:]], axis=-1)
    q, k, v = _proj1(xf, row(norm_pre_mix[1]), qkv_layout(odd_w_qkv[0].astype(BF16)),
                     qkv_layout(row(odd_b_qkv[0])), cos_t, sin_t)
    attn = _swa_attention(q, k, v, odd_sinks[0], batch, seq)
    xf = _mix_mlp([attn], odd_w_o[0].astype(BF16), row(odd_b_o[0]), row(norm_post_mix[1]), xf,
                  row(norm_pre_mlp[1]), w1, w2, row(norm_post_mlp[1]), layer=1)
    return xf.reshape(batch, seq, d)
```

```python
import functools
import math

import jax
import jax.numpy as jnp
from jax import lax
from jax.experimental import pallas as pl
from jax.experimental.pallas import tpu as pltpu

D_MODEL = 1024
HEAD_DIM = 64
ROT_HALF = 8
ROPE_THETA = 500000.0
RMS_EPS = 1e-6
CONV_WIDTH = 512
DIFF_HEADS = 4
DIFF_WIDTH = 512
DIFF_EPS = 1e-5
SWA_HEADS = 16
SWA_GROUP = 8
SWA_KV_HEADS = 2
WINDOW = 128
D_FF = 4096
LANES = 128
SUBLANES = 8
LOG2E = 1.4426950408889634
Q_SCALE = HEAD_DIM ** -0.5 * LOG2E

VMEM_LIMIT = 56 * 1024 * 1024

BF16 = jnp.bfloat16
F32 = jnp.float32
NT_DIMS = (((1,), (1,)), ((), ()))


def _params(*sem):
    return pltpu.CompilerParams(dimension_semantics=sem, vmem_limit_bytes=VMEM_LIMIT)


def _resident(shape):
    return pl.BlockSpec(shape, lambda *_: (0,) * len(shape), pipeline_mode=pl.Buffered(1))


def _rmsnorm(x, w, eps):
    return x * lax.rsqrt(jnp.mean(x * x, axis=-1, keepdims=True) + eps) * w


def _pair_layout(w):
    shape = w.shape
    w = w.reshape(shape[:-1] + (shape[-1] // LANES, LANES))
    w = jnp.concatenate([w[..., 0:8], w[..., 64:72], w[..., 16:64], w[..., 8:16], w[..., 72:]],
                        axis=-1)
    return w.reshape(shape)


def _first_head_lanes(shape):
    lane = lax.broadcasted_iota(jnp.int32, shape, 1)
    return (lane < ROT_HALF) | ((lane >= 2 * ROT_HALF) & (lane < HEAD_DIM + ROT_HALF))


def _rope(x, cos_t, sin_t):
    lane = lax.broadcasted_iota(jnp.int32, x.shape, 1)
    return x * cos_t + pltpu.roll(x, HEAD_DIM, 1) * jnp.where(lane < HEAD_DIM, -sin_t, sin_t)


def _rope_blocks(x, cos_t, sin_t, scale=None):
    out = []
    for j in range(x.shape[1] // LANES):
        r = _rope(x[:, j * LANES:(j + 1) * LANES], cos_t, sin_t)
        out.append((r if scale is None else r * scale).astype(BF16))
    return jnp.concatenate(out, axis=1)


def _rope_table_kernel(invf_ref, pos_ref, cos_ref, sin_ref):
    pos = pos_ref[...].astype(F32)
    for f in range(ROT_HALF):
        ang = pos * invf_ref[f]
        cos_ref[f] = jnp.cos(ang)
        sin_ref[f] = jnp.sin(ang)


def _rope_tables(positions):
    n = positions.size
    inv_freq = ROPE_THETA ** (-jnp.arange(0, 2 * ROT_HALF, 2, dtype=F32) / (2 * ROT_HALF))
    pos2d = positions.reshape(n // LANES, LANES)
    cos_c, sin_c = pl.pallas_call(
        _rope_table_kernel,
        out_shape=[jax.ShapeDtypeStruct((ROT_HALF, n // LANES, LANES), F32)] * 2,
        in_specs=[pl.BlockSpec(memory_space=pltpu.SMEM),
                  pl.BlockSpec(memory_space=pltpu.VMEM)],
        out_specs=[pl.BlockSpec(memory_space=pltpu.VMEM)] * 2,
        name="rope_tables",
    )(inv_freq, pos2d)

    rotary = (jnp.arange(LANES) % HEAD_DIM) < 2 * ROT_HALF

    def expand(t, fill):
        t = t.reshape(ROT_HALF, n).T
        return jnp.where(rotary, jnp.tile(t, (1, LANES // ROT_HALF)), fill)

    return expand(cos_c, 1.0), expand(sin_c, 0.0)


def _proj0_kernel(x_ref, nw_ref, w_ref, cos_ref, sin_ref, cw_ref,
                  conv_ref, q_ref, k_ref, v_ref, carry_ref, *, tiles_per_seq):
    i = pl.program_id(0)
    tm = x_ref.shape[0]
    c = CONV_WIDTH

    @pl.when(i == 0)
    def _():
        carry_ref[...] = jnp.zeros_like(carry_ref)

    hb = _rmsnorm(x_ref[...], nw_ref[...], RMS_EPS).astype(BF16)

    def proj(block):
        return jnp.dot(hb, w_ref[:, block * c:(block + 1) * c], preferred_element_type=F32)

    cos_t = cos_ref[...]
    sin_t = sin_ref[...]
    q_ref[...] = _rope_blocks(proj(3), cos_t, sin_t, Q_SCALE)
    u = proj(1) * proj(2)
    k_ref[...] = _rope_blocks(proj(4), cos_t, sin_t)
    gb = proj(0)

    prev = jnp.where(i % tiles_per_seq == 0, 0.0, carry_ref[...])
    row = lax.broadcasted_iota(jnp.int32, u.shape, 0)
    u1 = jnp.where(row == 0, prev[SUBLANES - 1:SUBLANES], pltpu.roll(u, 1, 0))
    u2 = jnp.where(row == 0, prev[SUBLANES - 2:SUBLANES - 1],
                   jnp.where(row == 1, prev[SUBLANES - 1:SUBLANES], pltpu.roll(u, 2, 0)))
    cw = cw_ref[...]
    conv = cw[0:1] * u2 + cw[1:2] * u1 + cw[2:3] * u
    conv_ref[...] = (gb * conv).astype(BF16)
    carry_ref[...] = u[tm - SUBLANES:, :]
    v_ref[...] = proj(5).astype(BF16)


def _proj0(x, nw, w, cos_t, sin_t, conv_w, seq, tm=1024):
    n = x.shape[0]
    row = lambda width: pl.BlockSpec((tm, width), lambda i: (i, 0))
    out = jax.ShapeDtypeStruct((n, CONV_WIDTH), BF16)
    return pl.pallas_call(
        functools.partial(_proj0_kernel, tiles_per_seq=seq // tm),
        grid=(n // tm,),
        in_specs=[row(D_MODEL), _resident((1, D_MODEL)), _resident(w.shape),
                  row(LANES), row(LANES), _resident(conv_w.shape)],
        out_specs=[row(CONV_WIDTH)] * 4,
        out_shape=[out] * 4,
        scratch_shapes=[pltpu.VMEM((SUBLANES, CONV_WIDTH), F32)],
        compiler_params=_params("arbitrary"),
        name="proj0",
    )(x, nw, w, cos_t, sin_t, conv_w)


def _diff_attn_kernel(lam_ref, q_ref, k_ref, v_ref, sw_ref, o_ref,
                      qs_ref, va_ref, m_ref, acc_ref, *, lam_init, chain_rows):
    qi = pl.program_id(2)
    tq = q_ref.shape[0]
    hw = 2 * HEAD_DIM
    heads = q_ref.shape[1] // hw

    @pl.when(qi == 0)
    def _():
        for h in range(heads):
            va_ref[:, 2 * h * hw:(2 * h + 1) * hw] = v_ref[:, h * hw:(h + 1) * hw]
            va_ref[:, (2 * h + 1) * hw:(2 * h + 2) * hw] = jnp.ones((va_ref.shape[0], hw), BF16)

    map1 = jnp.where(_first_head_lanes((1, hw)), 1.0, 0.0).astype(BF16)
    for h in range(heads):
        q = q_ref[:, h * hw:(h + 1) * hw]
        qs_ref[2 * h * tq:(2 * h + 1) * tq, :] = q * map1
        qs_ref[(2 * h + 1) * tq:(2 * h + 2) * tq, :] = q * (1.0 - map1)

    chains = [slice(r0, r0 + chain_rows) for r0 in range(0, heads * 2 * tq, chain_rows)]
    cr = chain_rows
    tri = (lax.broadcasted_iota(jnp.int32, (cr, cr), 1)
           <= lax.broadcasted_iota(jnp.int32, (cr, cr), 0))

    def step(start, diagonal, first):
        widths = [(rows.start % tq) + cr if diagonal else tq for rows in chains]
        head_of = [rows.start // (2 * tq) for rows in chains]
        scores = [lax.dot_general(qs_ref[rows, :],
                                  k_ref[pl.ds(start, w), h * hw:(h + 1) * hw], NT_DIMS,
                                  preferred_element_type=F32)
                  for rows, w, h in zip(chains, widths, head_of)]
        for rows, w, h, s in zip(chains, widths, head_of, scores):
            if diagonal:
                last = jnp.where(tri, s[:, w - cr:], -jnp.inf)
                s = last if w == cr else jnp.concatenate([s[:, :w - cr], last], axis=1)
            m_new = jnp.max(s, axis=-1, keepdims=True)
            if first:
                m_new = jnp.broadcast_to(m_new, (cr, LANES))
            else:
                m_prev = m_ref[rows, :]
                m_new = jnp.maximum(m_prev, m_new)
            p = jnp.exp2(s - jnp.concatenate([m_new] * (w // LANES), axis=1)).astype(BF16)
            pv = jnp.dot(p, va_ref[pl.ds(start, w), 2 * h * hw:(2 * h + 2) * hw],
                         preferred_element_type=F32)
            if not first:
                alpha = jnp.exp2(m_prev - m_new)
                pv = jnp.concatenate([alpha, alpha], axis=1) * acc_ref[rows, :] + pv
            acc_ref[rows, :] = pv
            m_ref[rows, :] = m_new

    @pl.when(qi == 0)
    def _():
        step(0, True, True)

    @pl.when(qi > 0)
    def _():
        step(0, False, True)

        def body(kb, carry):
            step(pl.multiple_of(kb * tq, tq), False, False)
            return carry

        lax.fori_loop(1, qi, body, 0)
        step(pl.multiple_of(qi * tq, tq), True, False)

    lam_v = lam_ref[...]
    lam = (jnp.exp(jnp.sum(lam_v[0:1] * lam_v[1:2], keepdims=True))
           - jnp.exp(jnp.sum(lam_v[2:3] * lam_v[3:4], keepdims=True)) + lam_init)
    for h in range(heads):
        acc = acc_ref[2 * h * tq:(2 * h + 2) * tq, :]
        o = acc[:, 0:hw] / acc[:, hw:]
        o = o[0:tq] - lam * o[tq:]
        o = _rmsnorm(o, sw_ref[...], DIFF_EPS) * (1.0 - lam_init)
        o_ref[:, h * hw:(h + 1) * hw] = o.astype(o_ref.dtype)


def _diff_attention(q, k, v, lam_vecs, subln_w, lam_init, batch, seq, tq=1024,
                    chain_rows=256, heads_per_step=2):
    q, k, v = (t.reshape(batch, seq, DIFF_WIDTH) for t in (q, k, v))
    hw = 2 * HEAD_DIM
    hps = heads_per_step
    out = pl.pallas_call(
        functools.partial(_diff_attn_kernel, lam_init=lam_init, chain_rows=chain_rows),
        grid=(batch, DIFF_HEADS // hps, seq // tq),
        in_specs=[_resident(lam_vecs.shape),
                  pl.BlockSpec((None, tq, hps * hw), lambda b, h, i: (b, i, h)),
                  pl.BlockSpec((None, seq, hps * hw), lambda b, h, i: (b, 0, h)),
                  pl.BlockSpec((None, seq, hps * hw), lambda b, h, i: (b, 0, h)),
                  _resident(subln_w.shape)],
        out_specs=pl.BlockSpec((None, tq, hps * hw), lambda b, h, i: (b, i, h)),
        out_shape=jax.ShapeDtypeStruct((batch, seq, DIFF_WIDTH), BF16),
        scratch_shapes=[pltpu.VMEM((hps * 2 * tq, hw), BF16),
                        pltpu.VMEM((seq, hps * 2 * hw), BF16),
                        pltpu.VMEM((hps * 2 * tq, hw), F32),
                        pltpu.VMEM((hps * 2 * tq, 2 * hw), F32)],
        compiler_params=_params("arbitrary", "arbitrary", "arbitrary"),
        name="diff_attn",
    )(lam_vecs, q, k, v, subln_w)
    return out.reshape(batch * seq, DIFF_WIDTH)


def _mix_mlp_kernel(*refs, n_in, has_bias, ff_chunk):
    a_refs = refs[:n_in]
    wo_ref = refs[n_in]
    b_ref = refs[n_in + 1] if has_bias else None
    nwo_ref, x_ref, nw1_ref, w1_ref, w2_ref, nw2_ref, o_ref = refs[n_in + 1 + has_bias:]
    a = jnp.concatenate([r[...] for r in a_refs], axis=1) if n_in > 1 else a_refs[0][...]
    h = jnp.dot(a, wo_ref[...], preferred_element_type=F32)
    if has_bias:
        h = h + b_ref[...]
    x1 = x_ref[...] + _rmsnorm(h, nwo_ref[...], RMS_EPS)

    hb = _rmsnorm(x1, nw1_ref[...], RMS_EPS).astype(BF16)
    acc = None
    for c0 in range(0, D_FF, ff_chunk):
        t = jnp.dot(hb, w1_ref[:, c0:c0 + ff_chunk], preferred_element_type=F32)
        t = jnp.square(jnp.maximum(t, 0.0)).astype(BF16)
        part = jnp.dot(t, w2_ref[c0:c0 + ff_chunk, :], preferred_element_type=F32)
        acc = part if acc is None else acc + part
    o_ref[...] = x1 + _rmsnorm(acc, nw2_ref[...], RMS_EPS)


def _mix_mlp(acts, wo, bias, nwo, x, nw1, w1, w2, nw2, layer, tm=512, ff_chunk=1024):
    n = x.shape[0]
    row = lambda width: pl.BlockSpec((tm, width), lambda i: (i, 0))
    stacked = lambda w: pl.BlockSpec((None,) + w.shape[1:], lambda i: (layer, 0, 0),
                                     pipeline_mode=pl.Buffered(1))
    has_bias = bias is not None
    operands = (list(acts) + [wo] + ([bias] if has_bias else [])
                + [nwo, x, nw1, w1, w2, nw2])
    in_specs = ([row(a.shape[1]) for a in acts] + [_resident(wo.shape)]
                + ([_resident(bias.shape)] if has_bias else [])
                + [_resident(nwo.shape), row(D_MODEL), _resident(nw1.shape),
                   stacked(w1), stacked(w2), _resident(nw2.shape)])
    return pl.pallas_call(
        functools.partial(_mix_mlp_kernel, n_in=len(acts), has_bias=has_bias, ff_chunk=ff_chunk),
        grid=(n // tm,),
        in_specs=in_specs,
        out_specs=row(D_MODEL),
        out_shape=jax.ShapeDtypeStruct((n, D_MODEL), F32),
        compiler_params=_params("arbitrary"),
        name="mix_mlp",
    )(*operands)


def _proj1_kernel(x_ref, nw_ref, w_ref, b_ref, cos_ref, sin_ref, q_ref, k_ref, v_ref):
    hb = _rmsnorm(x_ref[...], nw_ref[...], RMS_EPS).astype(BF16)

    def proj(c0, width):
        return (jnp.dot(hb, w_ref[:, c0:c0 + width], preferred_element_type=F32)
                + b_ref[:, c0:c0 + width])

    cos_t = cos_ref[...]
    sin_t = sin_ref[...]
    nq = SWA_HEADS * HEAD_DIM
    kv = proj(nq, 2 * LANES)
    chunk = 2 * LANES
    for c0 in range(0, nq, chunk):
        q_ref[:, c0:c0 + chunk] = _rope_blocks(proj(c0, chunk), cos_t, sin_t, Q_SCALE)

    k = _rope(kv[:, 0:LANES], cos_t, sin_t)
    lane = lax.broadcasted_iota(jnp.int32, k.shape, 1)
    head_a = _first_head_lanes(k.shape)
    far = pltpu.roll(k, HEAD_DIM, 1)
    k0 = jnp.where(head_a, k, jnp.where(lane >= HEAD_DIM + 2 * ROT_HALF, far,
                                        pltpu.roll(k, ROT_HALF, 1)))
    k1 = jnp.where(head_a, jnp.where((lane >= 2 * ROT_HALF) & (lane < HEAD_DIM), far,
                                     pltpu.roll(k, LANES - ROT_HALF, 1)), k)
    k_ref[...] = jnp.concatenate([k0, k1], axis=1).astype(BF16)

    v = kv[:, LANES:]
    swapped = pltpu.roll(v, HEAD_DIM, 1)
    low = lane < HEAD_DIM
    v_ref[...] = jnp.concatenate([jnp.where(low, v, swapped), jnp.where(low, swapped, v)],
                                 axis=1).astype(BF16)


def _proj1(x, nw, w, b, cos_t, sin_t, tm=1024):
    n = x.shape[0]
    row = lambda width: pl.BlockSpec((tm, width), lambda i: (i, 0))
    nq = SWA_HEADS * HEAD_DIM
    return pl.pallas_call(
        _proj1_kernel,
        grid=(n // tm,),
        in_specs=[row(D_MODEL), _resident(nw.shape), _resident(w.shape), _resident(b.shape),
                  row(LANES), row(LANES)],
        out_specs=[row(nq), row(2 * LANES), row(2 * LANES)],
        out_shape=[jax.ShapeDtypeStruct((n, nq), BF16),
                   jax.ShapeDtypeStruct((n, 2 * LANES), BF16),
                   jax.ShapeDtypeStruct((n, 2 * LANES), BF16)],
        compiler_params=_params("arbitrary"),
        name="proj1",
    )(x, nw, w, b, cos_t, sin_t)


def _swa_kernel(sink_ref, q_ref, kp_ref, kc_ref, vp_ref, vc_ref, o_ref, *, blocks):
    n = pl.program_id(1)
    w = WINDOW
    i = lax.broadcasted_iota(jnp.int32, (w, 2 * w), 0)
    j = lax.broadcasted_iota(jnp.int32, (w, 2 * w), 1)
    dist = i + w - j
    band = (dist >= 0) & (dist < w)
    bias_first = jnp.where(band & ((n > 0) | (j >= w)), 0.0, -jnp.inf)
    bias_rest = jnp.where(band, 0.0, -jnp.inf)
    low = lax.broadcasted_iota(jnp.int32, (w, LANES), 1) < HEAD_DIM
    head_a = _first_head_lanes((w, LANES))
    ones = jnp.ones((2 * w, LANES), BF16)
    pairs = SWA_GROUP // 2

    def window(prev_ref, cur_ref, t, cols):
        if t == 0:
            return jnp.concatenate([prev_ref[:, cols], cur_ref[0:w, cols]], axis=0)
        return cur_ref[(t - 1) * w:(t + 1) * w, cols]

    chains = [(t, kh) for t in range(blocks) for kh in range(SWA_KV_HEADS)]
    scores = []
    for t, kh in chains:
        cols = slice(kh * LANES, (kh + 1) * LANES)
        rows = []
        for p in range(pairs):
            c0 = (kh * pairs + p) * LANES
            q2 = q_ref[t * w:(t + 1) * w, c0:c0 + LANES].astype(F32)
            rows.append(jnp.where(head_a, q2, 0.0).astype(BF16))
            rows.append(jnp.where(head_a, 0.0, q2).astype(BF16))
        qs = jnp.concatenate(rows, axis=0)
        scores.append(lax.dot_general(qs, window(kp_ref, kc_ref, t, cols), NT_DIMS,
                                      preferred_element_type=F32))

    for (t, kh), s in zip(chains, scores):
        cols = slice(kh * LANES, (kh + 1) * LANES)
        bias = bias_first if t == 0 else bias_rest
        va = jnp.concatenate([window(vp_ref, vc_ref, t, cols), ones], axis=1)
        probs = []
        sink_terms = []
        for g in range(SWA_GROUP):
            sink = sink_ref[kh * SWA_GROUP + g] * LOG2E
            sg = s[g * w:(g + 1) * w] + bias
            m = jnp.maximum(jnp.max(sg, axis=-1, keepdims=True), sink)
            m = jnp.broadcast_to(m, (w, LANES))
            probs.append(jnp.exp2(sg - jnp.concatenate([m, m], axis=1)).astype(BF16))
            sink_terms.append(jnp.exp2(sink - m))
        pv = jnp.dot(jnp.concatenate(probs, axis=0), va, preferred_element_type=F32)

        outs = []
        for g in range(SWA_GROUP):
            blk = pv[g * w:(g + 1) * w]
            outs.append(blk[:, 0:LANES] / (blk[:, LANES:] + sink_terms[g]))
        for p in range(pairs):
            c0 = (kh * pairs + p) * LANES
            o_ref[t * w:(t + 1) * w, c0:c0 + LANES] = jnp.where(
                low, outs[2 * p], outs[2 * p + 1]).astype(o_ref.dtype)


def _swa_attention(q, k, v, sinks, batch, seq, blocks=8):
    nq = SWA_HEADS * HEAD_DIM
    q = q.reshape(batch, seq, nq)
    k = k.reshape(batch, seq, 2 * LANES)
    v = v.reshape(batch, seq, 2 * LANES)
    w = WINDOW
    rows = blocks * w
    cur = pl.BlockSpec((None, rows, 2 * LANES), lambda b, n: (b, n, 0))
    prev = pl.BlockSpec((None, w, 2 * LANES), lambda b, n: (b, jnp.maximum(blocks * n - 1, 0), 0))
    out = pl.pallas_call(
        functools.partial(_swa_kernel, blocks=blocks),
        grid=(batch, seq // rows),
        in_specs=[pl.BlockSpec(memory_space=pltpu.SMEM),
                  pl.BlockSpec((None, rows, nq), lambda b, n: (b, n, 0)),
                  prev, cur, prev, cur],
        out_specs=pl.BlockSpec((None, rows, nq), lambda b, n: (b, n, 0)),
        out_shape=jax.ShapeDtypeStruct((batch, seq, nq), BF16),
        compiler_params=_params("arbitrary", "arbitrary"),
        name="swa_attn",
    )(sinks, q, k, k, v, v)
    return out.reshape(batch * seq, nq)


def kernel(x, positions, norm_pre_mix, norm_post_mix, norm_pre_mlp, norm_post_mlp, even_w_in, even_conv_w, even_lambda_q1, even_lambda_k1, even_lambda_q2, even_lambda_k2, even_subln_w, even_w_out, odd_w_qkv, odd_b_qkv, odd_sinks, odd_w_o, odd_b_o, mlp_w1, mlp_w2):
    batch, seq, d = x.shape
    n = batch * seq
    xf = x.reshape(n, d)
    cos_t, sin_t = _rope_tables(positions)
    row = lambda t: t.reshape(1, -1)

    lam_init = 0.8 - 0.6 * math.exp(-0.3 * 0)
    w1 = mlp_w1.astype(BF16)
    w2 = mlp_w2.astype(BF16)
    qk0 = slice(3 * CONV_WIDTH, 3 * CONV_WIDTH + 2 * DIFF_WIDTH)
    w_in = even_w_in[0].astype(BF16)
    w_in = jnp.concatenate([w_in[:, :qk0.start], _pair_layout(w_in[:, qk0]), w_in[:, qk0.stop:]],
                           axis=1)
    conv, q, k, v = _proj0(xf, row(norm_pre_mix[0]), w_in, cos_t, sin_t, even_conv_w[0], seq)
    lam_vecs = jnp.stack([even_lambda_q1[0], even_lambda_k1[0],
                          even_lambda_q2[0], even_lambda_k2[0]]).astype(F32)
    diff = _diff_attention(q, k, v, lam_vecs, row(even_subln_w[0]), lam_init, batch, seq)
    xf = _mix_mlp([conv, diff], even_w_out[0].astype(BF16), None, row(norm_post_mix[0]), xf,
                  row(norm_pre_mlp[0]), w1, w2, row(norm_post_mlp[0]), layer=0)

    nqk = (SWA_HEADS + SWA_KV_HEADS) * HEAD_DIM
    qkv_layout = lambda t: jnp.concatenate([_pair_layout(t[..., :nqk]), t[..., nqk:]], axis=-1)
    q, k, v = _proj1(xf, row(norm_pre_mix[1]), qkv_layout(odd_w_qkv[0].astype(BF16)),
                     qkv_layout(row(odd_b_qkv[0])), cos_t, sin_t)
    attn = _swa_attention(q, k, v, odd_sinks[0], batch, seq)
    xf = _mix_mlp([attn], odd_w_o[0].astype(BF16), row(odd_b_o[0]), row(norm_post_mix[1]), xf,
                  row(norm_pre_mlp[1]), w1, w2, row(norm_post_mlp[1]), layer=1)
    return xf.reshape(batch, seq, d)
```

```python
import functools
import math

import jax
import jax.numpy as jnp
from jax import lax
from jax.experimental import pallas as pl
from jax.experimental.pallas import tpu as pltpu

D_MODEL = 1024
HEAD_DIM = 64
ROT_HALF = 8
ROPE_THETA = 500000.0
RMS_EPS = 1e-6
CONV_WIDTH = 512
DIFF_HEADS = 4
DIFF_WIDTH = 512
DIFF_EPS = 1e-5
SWA_HEADS = 16
SWA_GROUP = 8
SWA_KV_HEADS = 2
WINDOW = 128
D_FF = 4096
LANES = 128
SUBLANES = 8
LOG2E = 1.4426950408889634
Q_SCALE = HEAD_DIM ** -0.5 * LOG2E

VMEM_LIMIT = 56 * 1024 * 1024

BF16 = jnp.bfloat16
F32 = jnp.float32
NT_DIMS = (((1,), (1,)), ((), ()))


def _params(*sem):
    return pltpu.CompilerParams(dimension_semantics=sem, vmem_limit_bytes=VMEM_LIMIT)


def _resident(shape):
    return pl.BlockSpec(shape, lambda *_: (0,) * len(shape), pipeline_mode=pl.Buffered(1))


def _rmsnorm(x, w, eps):
    return x * lax.rsqrt(jnp.mean(x * x, axis=-1, keepdims=True) + eps) * w


def _pair_layout(w):
    shape = w.shape
    w = w.reshape(shape[:-1] + (shape[-1] // LANES, LANES))
    w = jnp.concatenate([w[..., 0:8], w[..., 64:72], w[..., 16:64], w[..., 8:16], w[..., 72:]],
                        axis=-1)
    return w.reshape(shape)


def _first_head_lanes(shape):
    lane = lax.broadcasted_iota(jnp.int32, shape, 1)
    return (lane < ROT_HALF) | ((lane >= 2 * ROT_HALF) & (lane < HEAD_DIM + ROT_HALF))


def _rope(x, cos_t, sin_t):
    lane = lax.broadcasted_iota(jnp.int32, x.shape, 1)
    return x * cos_t + pltpu.roll(x, HEAD_DIM, 1) * jnp.where(lane < HEAD_DIM, -sin_t, sin_t)


def _rope_blocks(x, cos_t, sin_t, scale=None):
    out = []
    for j in range(x.shape[1] // LANES):
        r = _rope(x[:, j * LANES:(j + 1) * LANES], cos_t, sin_t)
        out.append((r if scale is None else r * scale).astype(BF16))
    return jnp.concatenate(out, axis=1)


def _rope_table_kernel(invf_ref, pos_ref, cos_ref, sin_ref):
    pos = pos_ref[...].astype(F32)
    for f in range(ROT_HALF):
        ang = pos * invf_ref[f]
        cos_ref[f] = jnp.cos(ang)
        sin_ref[f] = jnp.sin(ang)


def _rope_tables(positions):
    n = positions.size
    inv_freq = ROPE_THETA ** (-jnp.arange(0, 2 * ROT_HALF, 2, dtype=F32) / (2 * ROT_HALF))
    pos2d = positions.reshape(n // LANES, LANES)
    cos_c, sin_c = pl.pallas_call(
        _rope_table_kernel,
        out_shape=[jax.ShapeDtypeStruct((ROT_HALF, n // LANES, LANES), F32)] * 2,
        in_specs=[pl.BlockSpec(memory_space=pltpu.SMEM),
                  pl.BlockSpec(memory_space=pltpu.VMEM)],
        out_specs=[pl.BlockSpec(memory_space=pltpu.VMEM)] * 2,
        name="rope_tables",
    )(inv_freq, pos2d)

    rotary = (jnp.arange(LANES) % HEAD_DIM) < 2 * ROT_HALF

    def expand(t, fill):
        t = t.reshape(ROT_HALF, n).T
        return jnp.where(rotary, jnp.tile(t, (1, LANES // ROT_HALF)), fill)

    return expand(cos_c, 1.0), expand(sin_c, 0.0)


def _proj0_kernel(x_ref, nw_ref, w_ref, cos_ref, sin_ref, cw_ref,
                  conv_ref, q_ref, k_ref, v_ref, carry_ref, *, tiles_per_seq):
    i = pl.program_id(0)
    tm = x_ref.shape[0]
    c = CONV_WIDTH

    @pl.when(i == 0)
    def _():
        carry_ref[...] = jnp.zeros_like(carry_ref)

    hb = _rmsnorm(x_ref[...], nw_ref[...], RMS_EPS).astype(BF16)

    def proj(block):
        return jnp.dot(hb, w_ref[:, block * c:(block + 1) * c], preferred_element_type=F32)

    cos_t = cos_ref[...]
    sin_t = sin_ref[...]
    q_ref[...] = _rope_blocks(proj(3), cos_t, sin_t, Q_SCALE)
    u = proj(1) * proj(2)
    k_ref[...] = _rope_blocks(proj(4), cos_t, sin_t)
    gb = proj(0)

    prev = jnp.where(i % tiles_per_seq == 0, 0.0, carry_ref[...])
    row = lax.broadcasted_iota(jnp.int32, u.shape, 0)
    u1 = jnp.where(row == 0, prev[SUBLANES - 1:SUBLANES], pltpu.roll(u, 1, 0))
    u2 = jnp.where(row == 0, prev[SUBLANES - 2:SUBLANES - 1],
                   jnp.where(row == 1, prev[SUBLANES - 1:SUBLANES], pltpu.roll(u, 2, 0)))
    cw = cw_ref[...]
    conv = cw[0:1] * u2 + cw[1:2] * u1 + cw[2:3] * u
    conv_ref[...] = (gb * conv).astype(BF16)
    carry_ref[...] = u[tm - SUBLANES:, :]
    v_ref[...] = proj(5).astype(BF16)


def _proj0(x, nw, w, cos_t, sin_t, conv_w, seq, tm=1024):
    n = x.shape[0]
    row = lambda width: pl.BlockSpec((tm, width), lambda i: (i, 0))
    out = jax.ShapeDtypeStruct((n, CONV_WIDTH), BF16)
    return pl.pallas_call(
        functools.partial(_proj0_kernel, tiles_per_seq=seq // tm),
        grid=(n // tm,),
        in_specs=[row(D_MODEL), _resident((1, D_MODEL)), _resident(w.shape),
                  row(LANES), row(LANES), _resident(conv_w.shape)],
        out_specs=[row(CONV_WIDTH)] * 4,
        out_shape=[out] * 4,
        scratch_shapes=[pltpu.VMEM((SUBLANES, CONV_WIDTH), F32)],
        compiler_params=_params("arbitrary"),
        name="proj0",
    )(x, nw, w, cos_t, sin_t, conv_w)


def _diff_attn_kernel(lam_ref, q_ref, k_ref, v_ref, sw_ref, o_ref,
                      qs_ref, va_ref, m_ref, acc_ref, *, lam_init, chain_rows):
    qi = pl.program_id(2)
    tq = q_ref.shape[0]
    hw = 2 * HEAD_DIM
    heads = q_ref.shape[1] // hw

    @pl.when(qi == 0)
    def _():
        for h in range(heads):
            va_ref[:, 2 * h * hw:(2 * h + 1) * hw] = v_ref[:, h * hw:(h + 1) * hw]
            va_ref[:, (2 * h + 1) * hw:(2 * h + 2) * hw] = jnp.ones((va_ref.shape[0], hw), BF16)

    lam_v = lam_ref[...]
    lam = (jnp.exp(jnp.sum(lam_v[0:1] * lam_v[1:2], keepdims=True))
           - jnp.exp(jnp.sum(lam_v[2:3] * lam_v[3:4], keepdims=True)) + lam_init)

    map1 = jnp.where(_first_head_lanes((1, hw)), 1.0, 0.0).astype(BF16)
    for h in range(heads):
        q = q_ref[:, h * hw:(h + 1) * hw]
        qs_ref[2 * h * tq:(2 * h + 1) * tq, :] = q * map1
        qs_ref[(2 * h + 1) * tq:(2 * h + 2) * tq, :] = q * (1.0 - map1)

    chains = [slice(r0, r0 + chain_rows) for r0 in range(0, heads * 2 * tq, chain_rows)]
    cr = chain_rows
    tri = (lax.broadcasted_iota(jnp.int32, (cr, cr), 1)
           <= lax.broadcasted_iota(jnp.int32, (cr, cr), 0))

    def step(start, diagonal, first):
        widths = [(rows.start % tq) + cr if diagonal else tq for rows in chains]
        head_of = [rows.start // (2 * tq) for rows in chains]
        scores = [lax.dot_general(qs_ref[rows, :],
                                  k_ref[pl.ds(start, w), h * hw:(h + 1) * hw], NT_DIMS,
                                  preferred_element_type=F32)
                  for rows, w, h in zip(chains, widths, head_of)]
        for rows, w, h, s in zip(chains, widths, head_of, scores):
            if diagonal:
                last = jnp.where(tri, s[:, w - cr:], -jnp.inf)
                s = last if w == cr else jnp.concatenate([s[:, :w - cr], last], axis=1)
            m_new = jnp.max(s, axis=-1, keepdims=True)
            if first:
                m_new = jnp.broadcast_to(m_new, (cr, LANES))
            else:
                m_prev = m_ref[rows, :]
                m_new = jnp.maximum(m_prev, m_new)
            p = jnp.exp2(s - jnp.concatenate([m_new] * (w // LANES), axis=1)).astype(BF16)
            pv = jnp.dot(p, va_ref[pl.ds(start, w), 2 * h * hw:(2 * h + 2) * hw],
                         preferred_element_type=F32)
            if not first:
                alpha = jnp.exp2(m_prev - m_new)
                pv = jnp.concatenate([alpha, alpha], axis=1) * acc_ref[rows, :] + pv
            acc_ref[rows, :] = pv
            m_ref[rows, :] = m_new

    @pl.when(qi == 0)
    def _():
        step(0, True, True)

    @pl.when(qi > 0)
    def _():
        step(0, False, True)

        def body(kb, carry):
            step(pl.multiple_of(kb * tq, tq), False, False)
            return carry

        lax.fori_loop(1, qi, body, 0)
        step(pl.multiple_of(qi * tq, tq), True, False)

    for h in range(heads):
        acc = acc_ref[2 * h * tq:(2 * h + 2) * tq, :]
        o = acc[:, 0:hw] / acc[:, hw:]
        o = o[0:tq] - lam * o[tq:]
        o = _rmsnorm(o, sw_ref[...], DIFF_EPS) * (1.0 - lam_init)
        o_ref[:, h * hw:(h + 1) * hw] = o.astype(o_ref.dtype)


def _diff_attention(q, k, v, lam_vecs, subln_w, lam_init, batch, seq, tq=1024,
                    chain_rows=256, heads_per_step=2):
    q, k, v = (t.reshape(batch, seq, DIFF_WIDTH) for t in (q, k, v))
    hw = 2 * HEAD_DIM
    hps = heads_per_step
    out = pl.pallas_call(
        functools.partial(_diff_attn_kernel, lam_init=lam_init, chain_rows=chain_rows),
        grid=(batch, DIFF_HEADS // hps, seq // tq),
        in_specs=[_resident(lam_vecs.shape),
                  pl.BlockSpec((None, tq, hps * hw), lambda b, h, i: (b, i, h)),
                  pl.BlockSpec((None, seq, hps * hw), lambda b, h, i: (b, 0, h)),
                  pl.BlockSpec((None, seq, hps * hw), lambda b, h, i: (b, 0, h)),
                  _resident(subln_w.shape)],
        out_specs=pl.BlockSpec((None, tq, hps * hw), lambda b, h, i: (b, i, h)),
        out_shape=jax.ShapeDtypeStruct((batch, seq, DIFF_WIDTH), BF16),
        scratch_shapes=[pltpu.VMEM((hps * 2 * tq, hw), BF16),
                        pltpu.VMEM((seq, hps * 2 * hw), BF16),
                        pltpu.VMEM((hps * 2 * tq, hw), F32),
                        pltpu.VMEM((hps * 2 * tq, 2 * hw), F32)],
        compiler_params=_params("arbitrary", "arbitrary", "arbitrary"),
        name="diff_attn",
    )(lam_vecs, q, k, v, subln_w)
    return out.reshape(batch * seq, DIFF_WIDTH)


def _mix_mlp_kernel(*refs, n_in, has_bias, ff_chunk):
    a_refs = refs[:n_in]
    wo_ref = refs[n_in]
    b_ref = refs[n_in + 1] if has_bias else None
    nwo_ref, x_ref, nw1_ref, w1_ref, w2_ref, nw2_ref, o_ref = refs[n_in + 1 + has_bias:]
    tm = x_ref.shape[0]
    x1_parts = []
    hb_parts = []
    for r0 in range(0, tm, tm // 2):
        rows = slice(r0, r0 + tm // 2)
        a = (jnp.concatenate([r[rows, :] for r in a_refs], axis=1) if n_in > 1
             else a_refs[0][rows, :])
        h = jnp.dot(a, wo_ref[...], preferred_element_type=F32)
        if has_bias:
            h = h + b_ref[...]
        x1_half = x_ref[rows, :] + _rmsnorm(h, nwo_ref[...], RMS_EPS)
        x1_parts.append(x1_half)
        hb_parts.append(_rmsnorm(x1_half, nw1_ref[...], RMS_EPS).astype(BF16))
    x1 = jnp.concatenate(x1_parts, axis=0)
    hb = jnp.concatenate(hb_parts, axis=0)
    acc = None
    for c0 in range(0, D_FF, ff_chunk):
        t = jnp.dot(hb, w1_ref[:, c0:c0 + ff_chunk], preferred_element_type=F32)
        t = jnp.square(jnp.maximum(t, 0.0)).astype(BF16)
        part = jnp.dot(t, w2_ref[c0:c0 + ff_chunk, :], preferred_element_type=F32)
        acc = part if acc is None else acc + part
    o_ref[...] = x1 + _rmsnorm(acc, nw2_ref[...], RMS_EPS)


def _mix_mlp(acts, wo, bias, nwo, x, nw1, w1, w2, nw2, layer, tm=512, ff_chunk=1024):
    n = x.shape[0]
    row = lambda width: pl.BlockSpec((tm, width), lambda i: (i, 0))
    stacked = lambda w: pl.BlockSpec((None,) + w.shape[1:], lambda i: (layer, 0, 0),
                                     pipeline_mode=pl.Buffered(1))
    has_bias = bias is not None
    operands = (list(acts) + [wo] + ([bias] if has_bias else [])
                + [nwo, x, nw1, w1, w2, nw2])
    in_specs = ([row(a.shape[1]) for a in acts] + [_resident(wo.shape)]
                + ([_resident(bias.shape)] if has_bias else [])
                + [_resident(nwo.shape), row(D_MODEL), _resident(nw1.shape),
                   stacked(w1), stacked(w2), _resident(nw2.shape)])
    return pl.pallas_call(
        functools.partial(_mix_mlp_kernel, n_in=len(acts), has_bias=has_bias, ff_chunk=ff_chunk),
        grid=(n // tm,),
        in_specs=in_specs,
        out_specs=row(D_MODEL),
        out_shape=jax.ShapeDtypeStruct((n, D_MODEL), F32),
        compiler_params=_params("arbitrary"),
        name="mix_mlp",
    )(*operands)


def _proj1_kernel(x_ref, nw_ref, w_ref, b_ref, cos_ref, sin_ref, q_ref, k_ref, v_ref):
    hb = _rmsnorm(x_ref[...], nw_ref[...], RMS_EPS).astype(BF16)

    def proj(c0, width):
        return (jnp.dot(hb, w_ref[:, c0:c0 + width], preferred_element_type=F32)
                + b_ref[:, c0:c0 + width])

    cos_t = cos_ref[...]
    sin_t = sin_ref[...]
    nq = SWA_HEADS * HEAD_DIM
    kv = proj(nq, 2 * LANES)
    chunk = 2 * LANES
    for c0 in range(0, nq, chunk):
        q_ref[:, c0:c0 + chunk] = _rope_blocks(proj(c0, chunk), cos_t, sin_t, Q_SCALE)

    k = _rope(kv[:, 0:LANES], cos_t, sin_t)
    lane = lax.broadcasted_iota(jnp.int32, k.shape, 1)
    head_a = _first_head_lanes(k.shape)
    far = pltpu.roll(k, HEAD_DIM, 1)
    k0 = jnp.where(head_a, k, jnp.where(lane >= HEAD_DIM + 2 * ROT_HALF, far,
                                        pltpu.roll(k, ROT_HALF, 1)))
    k1 = jnp.where(head_a, jnp.where((lane >= 2 * ROT_HALF) & (lane < HEAD_DIM), far,
                                     pltpu.roll(k, LANES - ROT_HALF, 1)), k)
    k_ref[...] = jnp.concatenate([k0, k1], axis=1).astype(BF16)

    v = kv[:, LANES:]
    swapped = pltpu.roll(v, HEAD_DIM, 1)
    low = lane < HEAD_DIM
    v_ref[...] = jnp.concatenate([jnp.where(low, v, swapped), jnp.where(low, swapped, v)],
                                 axis=1).astype(BF16)


def _proj1(x, nw, w, b, cos_t, sin_t, tm=1024):
    n = x.shape[0]
    row = lambda width: pl.BlockSpec((tm, width), lambda i: (i, 0))
    nq = SWA_HEADS * HEAD_DIM
    return pl.pallas_call(
        _proj1_kernel,
        grid=(n // tm,),
        in_specs=[row(D_MODEL), _resident(nw.shape), _resident(w.shape), _resident(b.shape),
                  row(LANES), row(LANES)],
        out_specs=[row(nq), row(2 * LANES), row(2 * LANES)],
        out_shape=[jax.ShapeDtypeStruct((n, nq), BF16),
                   jax.ShapeDtypeStruct((n, 2 * LANES), BF16),
                   jax.ShapeDtypeStruct((n, 2 * LANES), BF16)],
        compiler_params=_params("arbitrary"),
        name="proj1",
    )(x, nw, w, b, cos_t, sin_t)


def _swa_kernel(sink_ref, q_ref, kp_ref, kc_ref, vp_ref, vc_ref, o_ref, *, blocks):
    n = pl.program_id(1)
    w = WINDOW
    i = lax.broadcasted_iota(jnp.int32, (w, 2 * w), 0)
    j = lax.broadcasted_iota(jnp.int32, (w, 2 * w), 1)
    dist = i + w - j
    band = (dist >= 0) & (dist < w)
    bias_first = jnp.where(band & ((n > 0) | (j >= w)), 0.0, -jnp.inf)
    bias_rest = jnp.where(band, 0.0, -jnp.inf)
    low = lax.broadcasted_iota(jnp.int32, (w, LANES), 1) < HEAD_DIM
    head_a = _first_head_lanes((w, LANES))
    ones = jnp.ones((2 * w, LANES), BF16)
    pairs = SWA_GROUP // 2

    def window(prev_ref, cur_ref, t, cols):
        if t == 0:
            return jnp.concatenate([prev_ref[:, cols], cur_ref[0:w, cols]], axis=0)
        return cur_ref[(t - 1) * w:(t + 1) * w, cols]

    chains = [(t, kh) for t in range(blocks) for kh in range(SWA_KV_HEADS)]
    scores = []
    for t, kh in chains:
        cols = slice(kh * LANES, (kh + 1) * LANES)
        rows = []
        for p in range(pairs):
            c0 = (kh * pairs + p) * LANES
            q2 = q_ref[t * w:(t + 1) * w, c0:c0 + LANES].astype(F32)
            rows.append(jnp.where(head_a, q2, 0.0).astype(BF16))
            rows.append(jnp.where(head_a, 0.0, q2).astype(BF16))
        qs = jnp.concatenate(rows, axis=0)
        scores.append(lax.dot_general(qs, window(kp_ref, kc_ref, t, cols), NT_DIMS,
                                      preferred_element_type=F32))

    for (t, kh), s in zip(chains, scores):
        cols = slice(kh * LANES, (kh + 1) * LANES)
        bias = bias_first if t == 0 else bias_rest
        va = jnp.concatenate([window(vp_ref, vc_ref, t, cols), ones], axis=1)
        probs = []
        sink_terms = []
        for g in range(SWA_GROUP):
            sink = sink_ref[kh * SWA_GROUP + g] * LOG2E
            sg = s[g * w:(g + 1) * w] + bias
            m = jnp.maximum(jnp.max(sg, axis=-1, keepdims=True), sink)
            m = jnp.broadcast_to(m, (w, LANES))
            probs.append(jnp.exp2(sg - jnp.concatenate([m, m], axis=1)).astype(BF16))
            sink_terms.append(jnp.exp2(sink - m))
        pv = jnp.dot(jnp.concatenate(probs, axis=0), va, preferred_element_type=F32)

        outs = []
        for g in range(SWA_GROUP):
            blk = pv[g * w:(g + 1) * w]
            outs.append(blk[:, 0:LANES] / (blk[:, LANES:] + sink_terms[g]))
        for p in range(pairs):
            c0 = (kh * pairs + p) * LANES
            o_ref[t * w:(t + 1) * w, c0:c0 + LANES] = jnp.where(
                low, outs[2 * p], outs[2 * p + 1]).astype(o_ref.dtype)


def _swa_attention(q, k, v, sinks, batch, seq, blocks=8):
    nq = SWA_HEADS * HEAD_DIM
    q = q.reshape(batch, seq, nq)
    k = k.reshape(batch, seq, 2 * LANES)
    v = v.reshape(batch, seq, 2 * LANES)
    w = WINDOW
    rows = blocks * w
    cur = pl.BlockSpec((None, rows, 2 * LANES), lambda b, n: (b, n, 0))
    prev = pl.BlockSpec((None, w, 2 * LANES), lambda b, n: (b, jnp.maximum(blocks * n - 1, 0), 0))
    out = pl.pallas_call(
        functools.partial(_swa_kernel, blocks=blocks),
        grid=(batch, seq // rows),
        in_specs=[pl.BlockSpec(memory_space=pltpu.SMEM),
                  pl.BlockSpec((None, rows, nq), lambda b, n: (b, n, 0)),
                  prev, cur, prev, cur],
        out_specs=pl.BlockSpec((None, rows, nq), lambda b, n: (b, n, 0)),
        out_shape=jax.ShapeDtypeStruct((batch, seq, nq), BF16),
        compiler_params=_params("arbitrary", "arbitrary"),
        name="swa_attn",
    )(sinks, q, k, k, v, v)
    return out.reshape(batch * seq, nq)


def kernel(x, positions, norm_pre_mix, norm_post_mix, norm_pre_mlp, norm_post_mlp, even_w_in, even_conv_w, even_lambda_q1, even_lambda_k1, even_lambda_q2, even_lambda_k2, even_subln_w, even_w_out, odd_w_qkv, odd_b_qkv, odd_sinks, odd_w_o, odd_b_o, mlp_w1, mlp_w2):
    batch, seq, d = x.shape
    n = batch * seq
    xf = x.reshape(n, d)
    cos_t, sin_t = _rope_tables(positions)
    row = lambda t: t.reshape(1, -1)

    lam_init = 0.8 - 0.6 * math.exp(-0.3 * 0)
    w1 = mlp_w1.astype(BF16)
    w2 = mlp_w2.astype(BF16)
    qk0 = slice(3 * CONV_WIDTH, 3 * CONV_WIDTH + 2 * DIFF_WIDTH)
    w_in = even_w_in[0].astype(BF16)
    w_in = jnp.concatenate([w_in[:, :qk0.start], _pair_layout(w_in[:, qk0]), w_in[:, qk0.stop:]],
                           axis=1)
    conv, q, k, v = _proj0(xf, row(norm_pre_mix[0]), w_in, cos_t, sin_t, even_conv_w[0], seq)
    lam_vecs = jnp.stack([even_lambda_q1[0], even_lambda_k1[0],
                          even_lambda_q2[0], even_lambda_k2[0]]).astype(F32)
    diff = _diff_attention(q, k, v, lam_vecs, row(even_subln_w[0]), lam_init, batch, seq)
    xf = _mix_mlp([conv, diff], even_w_out[0].astype(BF16), None, row(norm_post_mix[0]), xf,
                  row(norm_pre_mlp[0]), w1, w2, row(norm_post_mlp[0]), layer=0)

    nqk = (SWA_HEADS + SWA_KV_HEADS) * HEAD_DIM
    qkv_layout = lambda t: jnp.concatenate([_pair_layout(t[..., :nqk]), t[..., nqk:]], axis=-1)
    q, k, v = _proj1(xf, row(norm_pre_mix[1]), qkv_layout(odd_w_qkv[0].astype(BF16)),
                     qkv_layout(row(odd_b_qkv[0])), cos_t, sin_t)
    attn = _swa_attention(q, k, v, odd_sinks[0], batch, seq)
    xf = _mix_mlp([attn], odd_w_o[0].astype(BF16), row(odd_b_o[0]), row(norm_post_mix[1]), xf,
                  row(norm_pre_mlp[1]), w1, w2, row(norm_post_mlp[1]), layer=1)
    return xf.reshape(batch, seq, d)
```

```python
import functools
import math

import jax
import jax.numpy as jnp
from jax import lax
from jax.experimental import pallas as pl
from jax.experimental.pallas import tpu as pltpu

D_MODEL = 1024
HEAD_DIM = 64
ROT_HALF = 8
ROPE_THETA = 500000.0
RMS_EPS = 1e-6
CONV_WIDTH = 512
DIFF_HEADS = 4
DIFF_WIDTH = 512
DIFF_EPS = 1e-5
SWA_HEADS = 16
SWA_GROUP = 8
SWA_KV_HEADS = 2
WINDOW = 128
D_FF = 4096
LANES = 128
SUBLANES = 8
LOG2E = 1.4426950408889634
Q_SCALE = HEAD_DIM ** -0.5 * LOG2E

VMEM_LIMIT = 56 * 1024 * 1024

BF16 = jnp.bfloat16
F32 = jnp.float32
NT_DIMS = (((1,), (1,)), ((), ()))


def _params(*sem):
    return pltpu.CompilerParams(dimension_semantics=sem, vmem_limit_bytes=VMEM_LIMIT)


def _resident(shape):
    return pl.BlockSpec(shape, lambda *_: (0,) * len(shape), pipeline_mode=pl.Buffered(1))


def _rmsnorm(x, w, eps):
    return x * lax.rsqrt(jnp.mean(x * x, axis=-1, keepdims=True) + eps) * w


def _pair_layout(w):
    shape = w.shape
    w = w.reshape(shape[:-1] + (shape[-1] // LANES, LANES))
    w = jnp.concatenate([w[..., 0:8], w[..., 64:72], w[..., 16:64], w[..., 8:16], w[..., 72:]],
                        axis=-1)
    return w.reshape(shape)


def _first_head_lanes(shape):
    lane = lax.broadcasted_iota(jnp.int32, shape, 1)
    return (lane < ROT_HALF) | ((lane >= 2 * ROT_HALF) & (lane < HEAD_DIM + ROT_HALF))


def _rope(x, cos_t, sin_t):
    lane = lax.broadcasted_iota(jnp.int32, x.shape, 1)
    return x * cos_t + pltpu.roll(x, HEAD_DIM, 1) * jnp.where(lane < HEAD_DIM, -sin_t, sin_t)


def _rope_blocks(x, cos_t, sin_t, scale=None):
    out = []
    for j in range(x.shape[1] // LANES):
        r = _rope(x[:, j * LANES:(j + 1) * LANES], cos_t, sin_t)
        out.append((r if scale is None else r * scale).astype(BF16))
    return jnp.concatenate(out, axis=1)


def _rope_table_kernel(invf_ref, pos_ref, cos_ref, sin_ref):
    pos = pos_ref[...].astype(F32)
    for f in range(ROT_HALF):
        ang = pos * invf_ref[f]
        cos_ref[f] = jnp.cos(ang)
        sin_ref[f] = jnp.sin(ang)


def _rope_tables(positions):
    n = positions.size
    inv_freq = ROPE_THETA ** (-jnp.arange(0, 2 * ROT_HALF, 2, dtype=F32) / (2 * ROT_HALF))
    pos2d = positions.reshape(n // LANES, LANES)
    cos_c, sin_c = pl.pallas_call(
        _rope_table_kernel,
        out_shape=[jax.ShapeDtypeStruct((ROT_HALF, n // LANES, LANES), F32)] * 2,
        in_specs=[pl.BlockSpec(memory_space=pltpu.SMEM),
                  pl.BlockSpec(memory_space=pltpu.VMEM)],
        out_specs=[pl.BlockSpec(memory_space=pltpu.VMEM)] * 2,
        name="rope_tables",
    )(inv_freq, pos2d)

    rotary = (jnp.arange(LANES) % HEAD_DIM) < 2 * ROT_HALF

    def expand(t, fill):
        t = t.reshape(ROT_HALF, n).T
        return jnp.where(rotary, jnp.tile(t, (1, LANES // ROT_HALF)), fill)

    return expand(cos_c, 1.0), expand(sin_c, 0.0)


def _proj0_kernel(x_ref, nw_ref, w_ref, cos_ref, sin_ref, cw_ref,
                  conv_ref, q_ref, k_ref, v_ref, carry_ref, *, tiles_per_seq):
    i = pl.program_id(0)
    tm = x_ref.shape[0]
    c = CONV_WIDTH

    @pl.when(i == 0)
    def _():
        carry_ref[...] = jnp.zeros_like(carry_ref)

    hb = _rmsnorm(x_ref[...], nw_ref[...], RMS_EPS).astype(BF16)

    def proj(block):
        return jnp.dot(hb, w_ref[:, block * c:(block + 1) * c], preferred_element_type=F32)

    cos_t = cos_ref[...]
    sin_t = sin_ref[...]
    q_ref[...] = _rope_blocks(proj(3), cos_t, sin_t, Q_SCALE)
    u = proj(1) * proj(2)
    k_ref[...] = _rope_blocks(proj(4), cos_t, sin_t)
    gb = proj(0)

    prev = jnp.where(i % tiles_per_seq == 0, 0.0, carry_ref[...])
    row = lax.broadcasted_iota(jnp.int32, u.shape, 0)
    u1 = jnp.where(row == 0, prev[SUBLANES - 1:SUBLANES], pltpu.roll(u, 1, 0))
    u2 = jnp.where(row == 0, prev[SUBLANES - 2:SUBLANES - 1],
                   jnp.where(row == 1, prev[SUBLANES - 1:SUBLANES], pltpu.roll(u, 2, 0)))
    cw = cw_ref[...]
    conv = cw[0:1] * u2 + cw[1:2] * u1 + cw[2:3] * u
    conv_ref[...] = (gb * conv).astype(BF16)
    carry_ref[...] = u[tm - SUBLANES:, :]
    v_ref[...] = proj(5).astype(BF16)


def _proj0(x, nw, w, cos_t, sin_t, conv_w, seq, tm=1024):
    n = x.shape[0]
    row = lambda width: pl.BlockSpec((tm, width), lambda i: (i, 0))
    out = jax.ShapeDtypeStruct((n, CONV_WIDTH), BF16)
    return pl.pallas_call(
        functools.partial(_proj0_kernel, tiles_per_seq=seq // tm),
        grid=(n // tm,),
        in_specs=[row(D_MODEL), _resident((1, D_MODEL)), _resident(w.shape),
                  row(LANES), row(LANES), _resident(conv_w.shape)],
        out_specs=[row(CONV_WIDTH)] * 4,
        out_shape=[out] * 4,
        scratch_shapes=[pltpu.VMEM((SUBLANES, CONV_WIDTH), F32)],
        compiler_params=_params("arbitrary"),
        name="proj0",
    )(x, nw, w, cos_t, sin_t, conv_w)


def _diff_attn_kernel(lam_ref, q_ref, k_ref, v_ref, sw_ref, o_ref,
                      qs_ref, va_ref, m_ref, acc_ref, *, lam_init, chain_rows):
    qi = pl.program_id(2)
    tq = q_ref.shape[0]
    hw = 2 * HEAD_DIM
    heads = q_ref.shape[1] // hw

    @pl.when(qi == 0)
    def _():
        for h in range(heads):
            va_ref[:, 2 * h * hw:(2 * h + 1) * hw] = v_ref[:, h * hw:(h + 1) * hw]
            va_ref[:, (2 * h + 1) * hw:(2 * h + 2) * hw] = jnp.ones((va_ref.shape[0], hw), BF16)

    lam_v = lam_ref[...]
    lam = (jnp.exp(jnp.sum(lam_v[0:1] * lam_v[1:2], keepdims=True))
           - jnp.exp(jnp.sum(lam_v[2:3] * lam_v[3:4], keepdims=True)) + lam_init)

    map1 = jnp.where(_first_head_lanes((1, hw)), 1.0, 0.0).astype(BF16)
    for h in range(heads):
        q = q_ref[:, h * hw:(h + 1) * hw]
        qs_ref[2 * h * tq:(2 * h + 1) * tq, :] = q * map1
        qs_ref[(2 * h + 1) * tq:(2 * h + 2) * tq, :] = q * (1.0 - map1)

    chains = [slice(r0, r0 + chain_rows) for r0 in range(0, heads * 2 * tq, chain_rows)]
    cr = chain_rows
    tri = (lax.broadcasted_iota(jnp.int32, (cr, cr), 1)
           <= lax.broadcasted_iota(jnp.int32, (cr, cr), 0))

    def step(start, diagonal, first):
        widths = [(rows.start % tq) + cr if diagonal else tq for rows in chains]
        head_of = [rows.start // (2 * tq) for rows in chains]
        scores = [lax.dot_general(qs_ref[rows, :],
                                  k_ref[pl.ds(start, w), h * hw:(h + 1) * hw], NT_DIMS,
                                  preferred_element_type=F32)
                  for rows, w, h in zip(chains, widths, head_of)]
        for rows, w, h, s in zip(chains, widths, head_of, scores):
            if diagonal:
                last = jnp.where(tri, s[:, w - cr:], -jnp.inf)
                s = last if w == cr else jnp.concatenate([s[:, :w - cr], last], axis=1)
            m_new = jnp.max(s, axis=-1, keepdims=True)
            if first:
                m_new = jnp.broadcast_to(m_new, (cr, LANES))
            else:
                m_prev = m_ref[rows, :]
                m_new = jnp.maximum(m_prev, m_new)
            p = jnp.exp2(s - jnp.concatenate([m_new] * (w // LANES), axis=1)).astype(BF16)
            pv = jnp.dot(p, va_ref[pl.ds(start, w), 2 * h * hw:(2 * h + 2) * hw],
                         preferred_element_type=F32)
            if not first:
                alpha = jnp.exp2(m_prev - m_new)
                pv = jnp.concatenate([alpha, alpha], axis=1) * acc_ref[rows, :] + pv
            acc_ref[rows, :] = pv
            m_ref[rows, :] = m_new

    @pl.when(qi == 0)
    def _():
        step(0, True, True)

    @pl.when(qi > 0)
    def _():
        step(0, False, True)

        def body(kb, carry):
            step(pl.multiple_of(kb * tq, tq), False, False)
            return carry

        lax.fori_loop(1, qi, body, 0)
        step(pl.multiple_of(qi * tq, tq), True, False)

    for h in range(heads):
        acc = acc_ref[2 * h * tq:(2 * h + 2) * tq, :]
        o = acc[:, 0:hw] / acc[:, hw:]
        o = o[0:tq] - lam * o[tq:]
        o = _rmsnorm(o, sw_ref[...], DIFF_EPS) * (1.0 - lam_init)
        o_ref[:, h * hw:(h + 1) * hw] = o.astype(o_ref.dtype)


def _diff_attention(q, k, v, lam_vecs, subln_w, lam_init, batch, seq, tq=1024,
                    chain_rows=256, heads_per_step=2):
    q, k, v = (t.reshape(batch, seq, DIFF_WIDTH) for t in (q, k, v))
    hw = 2 * HEAD_DIM
    hps = heads_per_step
    out = pl.pallas_call(
        functools.partial(_diff_attn_kernel, lam_init=lam_init, chain_rows=chain_rows),
        grid=(batch, DIFF_HEADS // hps, seq // tq),
        in_specs=[_resident(lam_vecs.shape),
                  pl.BlockSpec((None, tq, hps * hw), lambda b, h, i: (b, i, h)),
                  pl.BlockSpec((None, seq, hps * hw), lambda b, h, i: (b, 0, h)),
                  pl.BlockSpec((None, seq, hps * hw), lambda b, h, i: (b, 0, h)),
                  _resident(subln_w.shape)],
        out_specs=pl.BlockSpec((None, tq, hps * hw), lambda b, h, i: (b, i, h)),
        out_shape=jax.ShapeDtypeStruct((batch, seq, DIFF_WIDTH), BF16),
        scratch_shapes=[pltpu.VMEM((hps * 2 * tq, hw), BF16),
                        pltpu.VMEM((seq, hps * 2 * hw), BF16),
                        pltpu.VMEM((hps * 2 * tq, hw), F32),
                        pltpu.VMEM((hps * 2 * tq, 2 * hw), F32)],
        compiler_params=_params("arbitrary", "arbitrary", "arbitrary"),
        name="diff_attn",
    )(lam_vecs, q, k, v, subln_w)
    return out.reshape(batch * seq, DIFF_WIDTH)


def _mix_mlp_kernel(*refs, n_in, has_bias, ff_chunk):
    a_refs = refs[:n_in]
    wo_ref = refs[n_in]
    b_ref = refs[n_in + 1] if has_bias else None
    nwo_ref, x_ref, nw1_ref, w1_ref, w2_ref, nw2_ref, o_ref = refs[n_in + 1 + has_bias:]
    tm = x_ref.shape[0]
    x1_parts = []
    hb_parts = []
    for r0 in range(0, tm, tm // 2):
        rows = slice(r0, r0 + tm // 2)
        a = (jnp.concatenate([r[rows, :] for r in a_refs], axis=1) if n_in > 1
             else a_refs[0][rows, :])
        h = jnp.dot(a, wo_ref[...], preferred_element_type=F32)
        if has_bias:
            h = h + b_ref[...]
        x1_half = x_ref[rows, :] + _rmsnorm(h, nwo_ref[...], RMS_EPS)
        x1_parts.append(x1_half)
        hb_parts.append(_rmsnorm(x1_half, nw1_ref[...], RMS_EPS).astype(BF16))
    x1 = jnp.concatenate(x1_parts, axis=0)
    hb = jnp.concatenate(hb_parts, axis=0)
    acc = None
    for c0 in range(0, D_FF, ff_chunk):
        w1c = w1_ref[:, c0:c0 + ff_chunk]
        if c0 == 0:
            t = jnp.concatenate([jnp.dot(part, w1c, preferred_element_type=F32)
                                 for part in hb_parts], axis=0)
        else:
            t = jnp.dot(hb, w1c, preferred_element_type=F32)
        t = jnp.square(jnp.maximum(t, 0.0)).astype(BF16)
        part = jnp.dot(t, w2_ref[c0:c0 + ff_chunk, :], preferred_element_type=F32)
        acc = part if acc is None else acc + part
    o_ref[...] = x1 + _rmsnorm(acc, nw2_ref[...], RMS_EPS)


def _mix_mlp(acts, wo, bias, nwo, x, nw1, w1, w2, nw2, layer, tm=512, ff_chunk=1024):
    n = x.shape[0]
    row = lambda width: pl.BlockSpec((tm, width), lambda i: (i, 0))
    stacked = lambda w: pl.BlockSpec((None,) + w.shape[1:], lambda i: (layer, 0, 0),
                                     pipeline_mode=pl.Buffered(1))
    has_bias = bias is not None
    operands = (list(acts) + [wo] + ([bias] if has_bias else [])
                + [nwo, x, nw1, w1, w2, nw2])
    in_specs = ([row(a.shape[1]) for a in acts] + [_resident(wo.shape)]
                + ([_resident(bias.shape)] if has_bias else [])
                + [_resident(nwo.shape), row(D_MODEL), _resident(nw1.shape),
                   stacked(w1), stacked(w2), _resident(nw2.shape)])
    return pl.pallas_call(
        functools.partial(_mix_mlp_kernel, n_in=len(acts), has_bias=has_bias, ff_chunk=ff_chunk),
        grid=(n // tm,),
        in_specs=in_specs,
        out_specs=row(D_MODEL),
        out_shape=jax.ShapeDtypeStruct((n, D_MODEL), F32),
        compiler_params=_params("arbitrary"),
        name="mix_mlp",
    )(*operands)


def _proj1_kernel(x_ref, nw_ref, w_ref, b_ref, cos_ref, sin_ref, q_ref, k_ref, v_ref):
    hb = _rmsnorm(x_ref[...], nw_ref[...], RMS_EPS).astype(BF16)

    def proj(c0, width):
        return (jnp.dot(hb, w_ref[:, c0:c0 + width], preferred_element_type=F32)
                + b_ref[:, c0:c0 + width])

    cos_t = cos_ref[...]
    sin_t = sin_ref[...]
    nq = SWA_HEADS * HEAD_DIM
    kv = proj(nq, 2 * LANES)
    chunk = 2 * LANES
    for c0 in range(0, nq, chunk):
        q_ref[:, c0:c0 + chunk] = _rope_blocks(proj(c0, chunk), cos_t, sin_t, Q_SCALE)

    k = _rope(kv[:, 0:LANES], cos_t, sin_t)
    lane = lax.broadcasted_iota(jnp.int32, k.shape, 1)
    head_a = _first_head_lanes(k.shape)
    far = pltpu.roll(k, HEAD_DIM, 1)
    k0 = jnp.where(head_a, k, jnp.where(lane >= HEAD_DIM + 2 * ROT_HALF, far,
                                        pltpu.roll(k, ROT_HALF, 1)))
    k1 = jnp.where(head_a, jnp.where((lane >= 2 * ROT_HALF) & (lane < HEAD_DIM), far,
                                     pltpu.roll(k, LANES - ROT_HALF, 1)), k)
    k_ref[...] = jnp.concatenate([k0, k1], axis=1).astype(BF16)

    v = kv[:, LANES:]
    swapped = pltpu.roll(v, HEAD_DIM, 1)
    low = lane < HEAD_DIM
    v_ref[...] = jnp.concatenate([jnp.where(low, v, swapped), jnp.where(low, swapped, v)],
                                 axis=1).astype(BF16)


def _proj1(x, nw, w, b, cos_t, sin_t, tm=1024):
    n = x.shape[0]
    row = lambda width: pl.BlockSpec((tm, width), lambda i: (i, 0))
    nq = SWA_HEADS * HEAD_DIM
    return pl.pallas_call(
        _proj1_kernel,
        grid=(n // tm,),
        in_specs=[row(D_MODEL), _resident(nw.shape), _resident(w.shape), _resident(b.shape),
                  row(LANES), row(LANES)],
        out_specs=[row(nq), row(2 * LANES), row(2 * LANES)],
        out_shape=[jax.ShapeDtypeStruct((n, nq), BF16),
                   jax.ShapeDtypeStruct((n, 2 * LANES), BF16),
                   jax.ShapeDtypeStruct((n, 2 * LANES), BF16)],
        compiler_params=_params("arbitrary"),
        name="proj1",
    )(x, nw, w, b, cos_t, sin_t)


def _swa_kernel(sink_ref, q_ref, kp_ref, kc_ref, vp_ref, vc_ref, o_ref, *, blocks):
    n = pl.program_id(1)
    w = WINDOW
    i = lax.broadcasted_iota(jnp.int32, (w, 2 * w), 0)
    j = lax.broadcasted_iota(jnp.int32, (w, 2 * w), 1)
    dist = i + w - j
    band = (dist >= 0) & (dist < w)
    bias_first = jnp.where(band & ((n > 0) | (j >= w)), 0.0, -jnp.inf)
    bias_rest = jnp.where(band, 0.0, -jnp.inf)
    low = lax.broadcasted_iota(jnp.int32, (w, LANES), 1) < HEAD_DIM
    head_a = _first_head_lanes((w, LANES))
    ones = jnp.ones((2 * w, LANES), BF16)
    pairs = SWA_GROUP // 2

    def window(prev_ref, cur_ref, t, cols):
        if t == 0:
            return jnp.concatenate([prev_ref[:, cols], cur_ref[0:w, cols]], axis=0)
        return cur_ref[(t - 1) * w:(t + 1) * w, cols]

    chains = [(t, kh) for t in range(blocks) for kh in range(SWA_KV_HEADS)]
    scores = []
    for t, kh in chains:
        cols = slice(kh * LANES, (kh + 1) * LANES)
        rows = []
        for p in range(pairs):
            c0 = (kh * pairs + p) * LANES
            q2 = q_ref[t * w:(t + 1) * w, c0:c0 + LANES].astype(F32)
            rows.append(jnp.where(head_a, q2, 0.0).astype(BF16))
            rows.append(jnp.where(head_a, 0.0, q2).astype(BF16))
        qs = jnp.concatenate(rows, axis=0)
        scores.append(lax.dot_general(qs, window(kp_ref, kc_ref, t, cols), NT_DIMS,
                                      preferred_element_type=F32))

    for (t, kh), s in zip(chains, scores):
        cols = slice(kh * LANES, (kh + 1) * LANES)
        bias = bias_first if t == 0 else bias_rest
        va = jnp.concatenate([window(vp_ref, vc_ref, t, cols), ones], axis=1)
        probs = []
        sink_terms = []
        for g in range(SWA_GROUP):
            sink = sink_ref[kh * SWA_GROUP + g] * LOG2E
            sg = s[g * w:(g + 1) * w] + bias
            m = jnp.maximum(jnp.max(sg, axis=-1, keepdims=True), sink)
            m = jnp.broadcast_to(m, (w, LANES))
            probs.append(jnp.exp2(sg - jnp.concatenate([m, m], axis=1)).astype(BF16))
            sink_terms.append(jnp.exp2(sink - m))
        pv = jnp.dot(jnp.concatenate(probs, axis=0), va, preferred_element_type=F32)

        outs = []
        for g in range(SWA_GROUP):
            blk = pv[g * w:(g + 1) * w]
            outs.append(blk[:, 0:LANES] / (blk[:, LANES:] + sink_terms[g]))
        for p in range(pairs):
            c0 = (kh * pairs + p) * LANES
            o_ref[t * w:(t + 1) * w, c0:c0 + LANES] = jnp.where(
                low, outs[2 * p], outs[2 * p + 1]).astype(o_ref.dtype)


def _swa_attention(q, k, v, sinks, batch, seq, blocks=8):
    nq = SWA_HEADS * HEAD_DIM
    q = q.reshape(batch, seq, nq)
    k = k.reshape(batch, seq, 2 * LANES)
    v = v.reshape(batch, seq, 2 * LANES)
    w = WINDOW
    rows = blocks * w
    cur = pl.BlockSpec((None, rows, 2 * LANES), lambda b, n: (b, n, 0))
    prev = pl.BlockSpec((None, w, 2 * LANES), lambda b, n: (b, jnp.maximum(blocks * n - 1, 0), 0))
    out = pl.pallas_call(
        functools.partial(_swa_kernel, blocks=blocks),
        grid=(batch, seq // rows),
        in_specs=[pl.BlockSpec(memory_space=pltpu.SMEM),
                  pl.BlockSpec((None, rows, nq), lambda b, n: (b, n, 0)),
                  prev, cur, prev, cur],
        out_specs=pl.BlockSpec((None, rows, nq), lambda b, n: (b, n, 0)),
        out_shape=jax.ShapeDtypeStruct((batch, seq, nq), BF16),
        compiler_params=_params("arbitrary", "arbitrary"),
        name="swa_attn",
    )(sinks, q, k, k, v, v)
    return out.reshape(batch * seq, nq)


def kernel(x, positions, norm_pre_mix, norm_post_mix, norm_pre_mlp, norm_post_mlp, even_w_in, even_conv_w, even_lambda_q1, even_lambda_k1, even_lambda_q2, even_lambda_k2, even_subln_w, even_w_out, odd_w_qkv, odd_b_qkv, odd_sinks, odd_w_o, odd_b_o, mlp_w1, mlp_w2):
    batch, seq, d = x.shape
    n = batch * seq
    xf = x.reshape(n, d)
    cos_t, sin_t = _rope_tables(positions)
    row = lambda t: t.reshape(1, -1)

    lam_init = 0.8 - 0.6 * math.exp(-0.3 * 0)
    w1 = mlp_w1.astype(BF16)
    w2 = mlp_w2.astype(BF16)
    qk0 = slice(3 * CONV_WIDTH, 3 * CONV_WIDTH + 2 * DIFF_WIDTH)
    w_in = even_w_in[0].astype(BF16)
    w_in = jnp.concatenate([w_in[:, :qk0.start], _pair_layout(w_in[:, qk0]), w_in[:, qk0.stop:]],
                           axis=1)
    conv, q, k, v = _proj0(xf, row(norm_pre_mix[0]), w_in, cos_t, sin_t, even_conv_w[0], seq)
    lam_vecs = jnp.stack([even_lambda_q1[0], even_lambda_k1[0],
                          even_lambda_q2[0], even_lambda_k2[0]]).astype(F32)
    diff = _diff_attention(q, k, v, lam_vecs, row(even_subln_w[0]), lam_init, batch, seq)
    xf = _mix_mlp([conv, diff], even_w_out[0].astype(BF16), None, row(norm_post_mix[0]), xf,
                  row(norm_pre_mlp[0]), w1, w2, row(norm_post_mlp[0]), layer=0)

    nqk = (SWA_HEADS + SWA_KV_HEADS) * HEAD_DIM
    qkv_layout = lambda t: jnp.concatenate([_pair_layout(t[..., :nqk]), t[..., nqk:]], axis=-1)
    q, k, v = _proj1(xf, row(norm_pre_mix[1]), qkv_layout(odd_w_qkv[0].astype(BF16)),
                     qkv_layout(row(odd_b_qkv[0])), cos_t, sin_t)
    attn = _swa_attention(q, k, v, odd_sinks[0], batch, seq)
    xf = _mix_mlp([attn], odd_w_o[0].astype(BF16), row(odd_b_o[0]), row(norm_post_mix[1]), xf,
                  row(norm_pre_mlp[1]), w1, w2, row(norm_post_mlp[1]), layer=1)
    return xf.reshape(batch, seq, d)
```
